```python
import math
import jax, jax.numpy as jnp
from jax import lax
import numpy as np

D_MODEL = 1024
BATCH = 16
SEQ = 2048
DEPTH = 1

CHUNK = 64
N_META = 16
Q_BLOCK = 128
SB_HEADS = 8
SB_HEAD_DIM = 64
SB_WIDTH = SB_HEADS * SB_HEAD_DIM
MLA_HEADS = 8
MLA_Q_LORA = 384
MLA_KV_LORA = 256
MLA_NOPE = 64
MLA_ROPE = 32
MLA_V = 64
MLA_WIDTH = MLA_HEADS * MLA_V
ROPE_THETA = 10000.0
N_BRANCH = 2
IN_COLS = 3 * SB_WIDTH + MLA_Q_LORA + MLA_KV_LORA + MLA_ROPE + N_BRANCH * D_MODEL
D_FF = 2816
EPS = 1e-6
NEG_INF = -1e30
PAD_CHUNK_ID = 2 ** 30

kernel_name = "hybrid_sb_mla_macaron_block"


def rmsnorm(x, g):
    xf = x.astype(jnp.float32)
    y = xf * lax.rsqrt(jnp.mean(xf * xf, axis=-1, keepdims=True) + EPS)
    return (y * g.astype(jnp.float32)).astype(x.dtype)


def swiglu(h, w_gate, w_up, w_down):
    return (jax.nn.silu(h @ w_gate) * (h @ w_up)) @ w_down


def rope(x, pos):
    half = x.shape[-1] // 2
    inv = ROPE_THETA ** (-jnp.arange(half, dtype=jnp.float32) / half)
    ang = pos[:, None] * inv[None, :]
    cos = jnp.cos(ang).astype(x.dtype)
    sin = jnp.sin(ang).astype(x.dtype)
    x1, x2 = x[..., :half], x[..., half:]
    return jnp.concatenate([x1 * cos - x2 * sin, x1 * sin + x2 * cos], axis=-1)


def to_heads(t, n_heads):
    b, l, _ = t.shape
    return t.reshape(b, l, n_heads, -1).transpose(0, 2, 1, 3)


def from_heads(t):
    b, h, l, d = t.shape
    return t.transpose(0, 2, 1, 3).reshape(b, l, h * d)


def chunk_end(pos):
    if pos < N_META:
        return N_META
    return N_META + ((pos - N_META) // CHUNK + 1) * CHUNK


def stick_breaking_attention(q, k, v):
    lp, d = q.shape[2], q.shape[3]
    scale = d ** -0.5
    outs = []
    for blk in range(lp // Q_BLOCK):
        q0, q1 = blk * Q_BLOCK, (blk + 1) * Q_BLOCK
        z = jnp.einsum('bhtd,bhsd->bhts', q[:, :, q0:q1], k[:, :, :q1]).astype(jnp.float32) * scale
        strict = jnp.arange(q1)[None, :] < jnp.arange(q0, q1)[:, None]
        log_1m = jnp.where(strict, jax.nn.log_sigmoid(-z), 0.0)
        csum = jnp.cumsum(log_1m, axis=-1)
        log_a = jax.nn.log_sigmoid(z) + csum[..., -1:] - csum
        a = jnp.where(strict, jnp.exp(log_a), 0.0)
        outs.append(jnp.einsum('bhts,bhsd->bhtd', a.astype(v.dtype), v[:, :, :q1]))
    return jnp.concatenate(outs, axis=2)


def latent_attention(q_nope, q_rope, k_nope, k_rope, v, cid):
    lp = q_nope.shape[2]
    scale = (MLA_NOPE + MLA_ROPE) ** -0.5
    outs = []
    for blk in range(lp // Q_BLOCK):
        q0, q1 = blk * Q_BLOCK, (blk + 1) * Q_BLOCK
        kend = min(lp, chunk_end(q1 - 1))
        s = (jnp.einsum('bhtd,bhsd->bhts', q_nope[:, :, q0:q1], k_nope[:, :, :kend])
             + jnp.einsum('bhtd,bsd->bhts', q_rope[:, :, q0:q1], k_rope[:, :kend]))
        s = s.astype(jnp.float32) * scale
        mask = cid[None, :kend] <= cid[q0:q1, None]
        p = jax.nn.softmax(jnp.where(mask, s, NEG_INF), axis=-1)
        outs.append(jnp.einsum('bhts,bhsd->bhtd', p.astype(v.dtype), v[:, :, :kend]))
    return jnp.concatenate(outs, axis=2)


def hybrid_mixer(u, w_in, b_gate, q_norm_g, w_uq, kv_norm_g, w_ukv, w_sb_o, w_mla_o, w_out):
    bsz, l, _ = u.shape
    lp = -(-l // Q_BLOCK) * Q_BLOCK
    up = jnp.pad(u, ((0, 0), (0, lp - l), (0, 0)))
    proj = up @ w_in
    sizes = [SB_WIDTH, SB_WIDTH, SB_WIDTH, MLA_Q_LORA, MLA_KV_LORA, MLA_ROPE, N_BRANCH * D_MODEL]
    idx = np.cumsum(sizes)[:-1].tolist()
    sb_q, sb_k, sb_v, c_q, c_kv, k_r, gate_pre = jnp.split(proj, idx, axis=-1)

    pos = jnp.arange(lp)
    cid = jnp.where(pos < N_META, 0, 1 + (pos - N_META) // CHUNK)
    cid = jnp.where(pos < l, cid, PAD_CHUNK_ID)
    posf = pos.astype(jnp.float32)

    y_sb = stick_breaking_attention(to_heads(sb_q, SB_HEADS), to_heads(sb_k, SB_HEADS),
                                    to_heads(sb_v, SB_HEADS))
    y_sb = from_heads(y_sb) @ w_sb_o

    q = jnp.einsum('bld,dhe->bhle', rmsnorm(c_q, q_norm_g), w_uq)
    q_nope, q_rope = q[..., :MLA_NOPE], rope(q[..., MLA_NOPE:], posf)
    kv = jnp.einsum('bld,dhe->bhle', rmsnorm(c_kv, kv_norm_g), w_ukv)
    k_nope, v = kv[..., :MLA_NOPE], kv[..., MLA_NOPE:]
    k_rope = rope(k_r, posf)
    y_mla = latent_attention(q_nope, q_rope, k_nope, k_rope, v, cid)
    y_mla = from_heads(y_mla) @ w_mla_o

    g = jax.nn.sigmoid(gate_pre + b_gate)
    g_sb, g_mla = g[..., :D_MODEL], g[..., D_MODEL:]
    out = (g_sb * y_sb + g_mla * y_mla) @ w_out
    return out[:, :l]


def setup_inputs(seed: int = 0) -> dict:
    key = jax.random.key(seed)
    ks = jax.random.split(key, 24)

    def w(k, shape, fan_in):
        return jax.random.normal(k, shape, jnp.float32) * fan_in ** -0.5

    def gain(k, n):
        return 1.0 + 0.02 * jax.random.normal(k, (DEPTH, n), jnp.float32)

    return {
        "x": jax.random.normal(ks[0], (BATCH, SEQ, D_MODEL), jnp.float32),
        "meta_tokens": jax.random.normal(ks[1], (N_META, D_MODEL), jnp.float32),
        "ffn1_pre_g": gain(ks[2], D_MODEL),
        "ffn1_w_gate": w(ks[3], (DEPTH, D_MODEL, D_FF), D_MODEL),
        "ffn1_w_up": w(ks[4], (DEPTH, D_MODEL, D_FF), D_MODEL),
        "ffn1_w_down": w(ks[5], (DEPTH, D_FF, D_MODEL), D_FF),
        "ffn1_post_g": gain(ks[6], D_MODEL),
        "mix_pre_g": gain(ks[7], D_MODEL),
        "w_in": w(ks[8], (DEPTH, D_MODEL, IN_COLS), D_MODEL),
        "b_gate": 0.02 * jax.random.normal(ks[9], (DEPTH, N_BRANCH * D_MODEL), jnp.float32),
        "q_norm_g": gain(ks[10], MLA_Q_LORA),
        "w_uq": w(ks[11], (DEPTH, MLA_Q_LORA, MLA_HEADS, MLA_NOPE + MLA_ROPE), MLA_Q_LORA),
        "kv_norm_g": gain(ks[12], MLA_KV_LORA),
        "w_ukv": w(ks[13], (DEPTH, MLA_KV_LORA, MLA_HEADS, MLA_NOPE + MLA_V), MLA_KV_LORA),
        "w_sb_o": w(ks[14], (DEPTH, SB_WIDTH, D_MODEL), SB_WIDTH),
        "w_mla_o": w(ks[15], (DEPTH, MLA_WIDTH, D_MODEL), MLA_WIDTH),
        "w_out": w(ks[16], (DEPTH, D_MODEL, D_MODEL), D_MODEL),
        "mix_post_g": gain(ks[17], D_MODEL),
        "ffn2_pre_g": gain(ks[18], D_MODEL),
        "ffn2_w_gate": w(ks[19], (DEPTH, D_MODEL, D_FF), D_MODEL),
        "ffn2_w_up": w(ks[20], (DEPTH, D_MODEL, D_FF), D_MODEL),
        "ffn2_w_down": w(ks[21], (DEPTH, D_FF, D_MODEL), D_FF),
        "ffn2_post_g": gain(ks[22], D_MODEL),
    }


def reference(x, meta_tokens, ffn1_pre_g, ffn1_w_gate, ffn1_w_up, ffn1_w_down, ffn1_post_g,
              mix_pre_g, w_in, b_gate, q_norm_g, w_uq, kv_norm_g, w_ukv, w_sb_o, w_mla_o,
              w_out, mix_post_g, ffn2_pre_g, ffn2_w_gate, ffn2_w_up, ffn2_w_down, ffn2_post_g):
    bsz = x.shape[0]
    meta = jnp.broadcast_to(meta_tokens[None].astype(x.dtype), (bsz, N_META, x.shape[-1]))
    h = jnp.concatenate([meta, x], axis=1)
    for l in range(DEPTH):
        f = swiglu(rmsnorm(h, ffn1_pre_g[l]), ffn1_w_gate[l], ffn1_w_up[l], ffn1_w_down[l])
        h = h + 0.5 * rmsnorm(f, ffn1_post_g[l])
        m = hybrid_mixer(rmsnorm(h, mix_pre_g[l]), w_in[l], b_gate[l], q_norm_g[l], w_uq[l],
                         kv_norm_g[l], w_ukv[l], w_sb_o[l], w_mla_o[l], w_out[l])
        h = h + rmsnorm(m, mix_post_g[l])
        f = swiglu(rmsnorm(h, ffn2_pre_g[l]), ffn2_w_gate[l], ffn2_w_up[l], ffn2_w_down[l])
        h = h + 0.5 * rmsnorm(f, ffn2_post_g[l])
    return h[:, N_META:]
```

```python
import functools

import jax
import jax.numpy as jnp
from jax import lax
from jax.experimental import pallas as pl
from jax.experimental.pallas import tpu as pltpu

D_MODEL = 1024
D_FF = 2816
N_META = 16
CHUNK = 64
SB_WIDTH = 512
MLA_HEADS = 8
MLA_Q_LORA = 384
MLA_KV_LORA = 256
MLA_NOPE = 64
MLA_ROPE = 32
MLA_V = 64
ROPE_THETA = 10000.0
EPS = 1e-6
NEG_INF = -1e30
HEAD_PAIR = 128
MLA_QK_PAD = 128
META_PAD = 128
VMEM_LIMIT = 56 * 1024 * 1024

F32 = jnp.float32
BF16 = jnp.bfloat16


def _bf(x):
    return x.astype(BF16)


def _dot(a, b):
    return jnp.dot(a, b, preferred_element_type=F32)


def _dot_nt(a, b):
    return lax.dot_general(a, b, (((1,), (1,)), ((), ())), preferred_element_type=F32)


def _rms(x, g):
    return x * lax.rsqrt(jnp.mean(x * x, axis=-1, keepdims=True) + EPS) * g


def _ffn(h, pre_g, wg_ref, wu_ref, wd_ref, post_g):
    xn = _bf(_rms(h, pre_g))
    g = _dot(xn, wg_ref[...])
    u = _dot(xn, wu_ref[...])
    a = _bf(g * jax.nn.sigmoid(g) * u)
    f = _dot(a, wd_ref[...])
    return h + 0.5 * _rms(f, post_g)


def _ffn_proj_kernel(x_ref, cs_ref, sn_ref, pre_g, wg, wu, wd, post_g, mix_g, w_a, qn_g, w_q,
                     kvn_g, w_kv, h1_ref, sbq_ref, sbk_ref, sbv_ref, qm_ref, km_ref, vm_ref):
    h1 = _ffn(x_ref[...], pre_g[...], wg, wu, wd, post_g[...])
    h1_ref[...] = h1
    u = _bf(_rms(h1, mix_g[...]))
    p = _dot(u, w_a[...])
    sbq_ref[...] = _bf(p[:, 0:512])
    sbk_ref[...] = _bf(p[:, 512:1024])
    sbv_ref[...] = _bf(p[:, 1024:1536])
    c_q = p[:, 1536:1920]
    c_kv = p[:, 1920:2176]
    kr = p[:, 2176:2304]
    kr_rot = p[:, 2304:2432]
    cs = cs_ref[...]
    sn = sn_ref[...]
    scale = (MLA_NOPE + MLA_ROPE) ** -0.5

    q2 = _dot(_bf(_rms(c_q, qn_g[...])), w_q[...])
    half = MLA_HEADS * MLA_QK_PAD
    for h in range(MLA_HEADS):
        lo, hi = h * MLA_QK_PAD, (h + 1) * MLA_QK_PAD
        qm_ref[:, lo:hi] = _bf((q2[:, lo:hi] * cs + q2[:, half + lo:half + hi] * sn) * scale)

    kv = _dot(_bf(_rms(c_kv, kvn_g[...])), w_kv[...])
    k_rope = kr * cs + kr_rot * sn
    for h in range(MLA_HEADS):
        lo, hi = h * MLA_QK_PAD, (h + 1) * MLA_QK_PAD
        km_ref[:, lo:hi] = _bf(kv[:, lo:hi] + k_rope)
    vm_ref[...] = _bf(kv[:, half:])


def _const_spec(shape):
    return pl.BlockSpec(shape, lambda *_: (0,) * len(shape), pipeline_mode=pl.Buffered(1))


def _ffn_proj_call(x2d, cs, sn, weights, tm):
    t = x2d.shape[0]
    n_pos_blocks = cs.shape[0] // tm
    row = lambda w: pl.BlockSpec((tm, w), lambda i: (i, 0))
    pos = pl.BlockSpec((tm, MLA_QK_PAD), lambda i: (i % n_pos_blocks, 0))
    in_specs = [row(D_MODEL), pos, pos] + [_const_spec(w.shape) for w in weights]
    widths = (D_MODEL, SB_WIDTH, SB_WIDTH, SB_WIDTH, MLA_HEADS * MLA_QK_PAD,
              MLA_HEADS * MLA_QK_PAD, MLA_HEADS * MLA_V)
    dtypes = (F32,) + (BF16,) * 6
    return pl.pallas_call(
        _ffn_proj_kernel,
        grid=(t // tm,),
        in_specs=in_specs,
        out_specs=[row(w) for w in widths],
        out_shape=[jax.ShapeDtypeStruct((t, w), d) for w, d in zip(widths, dtypes)],
        compiler_params=pltpu.CompilerParams(dimension_semantics=("parallel",),
                                             vmem_limit_bytes=VMEM_LIMIT),
        name="ffn1_proj",
    )(x2d, cs, sn, *weights)


def _sb_kernel(q_ref, k_ref, v_ref, mk_ref, mv_ref, tri_ref, o_ref, *, tq):
    qi = pl.program_id(2)
    lane = lax.broadcasted_iota(jnp.int32, (tq, HEAD_PAIR), 1)
    row = lax.broadcasted_iota(jnp.int32, (tq, tq), 0)
    col = lax.broadcasted_iota(jnp.int32, (tq, tq), 1)
    strict = col < row
    meta_mask = lax.broadcasted_iota(jnp.int32, (tq, META_PAD), 1) < N_META
    q = q_ref[...]
    tri = tri_ref[...]

    def block(qh, kblk, vblk, tri_blk, rest, acc, mask):
        z = _dot_nt(qh, kblk)
        log_1m = jnp.minimum(-z, 0.0) - jnp.log(1.0 + jnp.exp(-jnp.abs(z)))
        log_b = z + log_1m
        if mask is not None:
            log_1m = jnp.where(mask, log_1m, 0.0)
        hi = _bf(log_1m)
        lo = _bf(log_1m - hi.astype(F32))
        suffix = _dot(hi, tri_blk) + _dot(lo, tri_blk)
        a = jnp.exp(log_b + suffix + rest)
        if mask is not None:
            a = jnp.where(mask, a, 0.0)
        acc = acc + _dot(_bf(a), vblk)
        rest = rest + jnp.sum(log_1m, axis=-1, keepdims=True)
        return rest, acc

    accs = []
    for hh in range(2):
        own = (lane < 64) if hh == 0 else (lane >= 64)
        qh = jnp.where(own, q, jnp.zeros_like(q))
        start = pl.multiple_of(qi * tq, tq)
        rest = jnp.zeros((tq, 1), F32)
        acc = jnp.zeros((tq, HEAD_PAIR), F32)
        rest, acc = block(qh, k_ref[pl.ds(start, tq), :], v_ref[pl.ds(start, tq), :], tri,
                          rest, acc, strict)

        def body(j, carry, qh=qh):
            s = pl.multiple_of((qi - 1 - j) * tq, tq)
            return block(qh, k_ref[pl.ds(s, tq), :], v_ref[pl.ds(s, tq), :], tri,
                         carry[0], carry[1], None)

        rest, acc = lax.fori_loop(0, qi, body, (rest, acc))
        rest, acc = block(qh, mk_ref[...], mv_ref[...], tri[:META_PAD, :META_PAD], rest, acc,
                          meta_mask)
        accs.append(acc)
    o_ref[...] = _bf(jnp.where(lane < 64, accs[0], accs[1]))


def _sb_call(sbq, sbk, sbv, mk, mv, bsz, seq, tq):
    nq = seq // tq
    tri = (lax.broadcasted_iota(jnp.int32, (tq, tq), 0)
           > lax.broadcasted_iota(jnp.int32, (tq, tq), 1)).astype(BF16)
    qspec = pl.BlockSpec((tq, HEAD_PAIR), lambda b, h, i: (b * nq + i, h))
    kvspec = pl.BlockSpec((seq, HEAD_PAIR), lambda b, h, i: (b, h))
    mspec = pl.BlockSpec((META_PAD, HEAD_PAIR), lambda b, h, i: (0, h))
    return pl.pallas_call(
        functools.partial(_sb_kernel, tq=tq),
        grid=(bsz, SB_WIDTH // HEAD_PAIR, nq),
        in_specs=[qspec, kvspec, kvspec, mspec, mspec,
                  pl.BlockSpec((tq, tq), lambda b, h, i: (0, 0))],
        out_specs=qspec,
        out_shape=jax.ShapeDtypeStruct(sbq.shape, BF16),
        compiler_params=pltpu.CompilerParams(
            dimension_semantics=("parallel", "parallel", "arbitrary"),
            vmem_limit_bytes=VMEM_LIMIT),
        name="sb_attention",
    )(sbq, sbk, sbv, mk, mv, tri)


def _mla_kernel(q_ref, k_ref, v_ref, mk_ref, mv_ref, o_ref, *, tq):
    qi = pl.program_id(2)
    lane = lax.broadcasted_iota(jnp.int32, (tq, HEAD_PAIR), 1)
    row = lax.broadcasted_iota(jnp.int32, (tq, tq), 0)
    col = lax.broadcasted_iota(jnp.int32, (tq, tq), 1)
    chunk_mask = (col // CHUNK) <= (row // CHUNK)
    meta_mask = lax.broadcasted_iota(jnp.int32, (tq, META_PAD), 1) < N_META

    def block(q, kblk, vblk, m, l, acc, mask):
        s = _dot_nt(q, kblk)
        if mask is not None:
            s = jnp.where(mask, s, NEG_INF)
        m_new = jnp.maximum(m, jnp.max(s, axis=-1, keepdims=True))
        alpha = jnp.exp(m - m_new)
        p = jnp.exp(s - m_new)
        l = alpha * l + jnp.sum(p, axis=-1, keepdims=True)
        acc = alpha * acc + _dot(_bf(p), vblk)
        return m_new, l, acc

    outs = []
    for hh in range(2):
        lo, hi = hh * MLA_QK_PAD, (hh + 1) * MLA_QK_PAD
        q = q_ref[:, lo:hi]
        m = jnp.full((tq, 1), NEG_INF, F32)
        l = jnp.zeros((tq, 1), F32)
        acc = jnp.zeros((tq, HEAD_PAIR), F32)
        m, l, acc = block(q, mk_ref[:, lo:hi], mv_ref[...], m, l, acc, meta_mask)

        def body(j, carry, q=q, lo=lo, hi=hi):
            s = pl.multiple_of(j * tq, tq)
            return block(q, k_ref[pl.ds(s, tq), lo:hi], v_ref[pl.ds(s, tq), :], *carry, None)

        m, l, acc = lax.fori_loop(0, qi, body, (m, l, acc))
        start = pl.multiple_of(qi * tq, tq)
        m, l, acc = block(q, k_ref[pl.ds(start, tq), lo:hi], v_ref[pl.ds(start, tq), :],
                          m, l, acc, chunk_mask)
        outs.append(acc / l)
    o_ref[...] = _bf(jnp.where(lane < 64, outs[0], outs[1]))


def _mla_call(qm, km, vm, mk, mv, bsz, seq, tq):
    nq = seq // tq
    pair = 2 * MLA_QK_PAD
    return pl.pallas_call(
        functools.partial(_mla_kernel, tq=tq),
        grid=(bsz, MLA_HEADS // 2, nq),
        in_specs=[pl.BlockSpec((tq, pair), lambda b, h, i: (b * nq + i, h)),
                  pl.BlockSpec((seq, pair), lambda b, h, i: (b, h)),
                  pl.BlockSpec((seq, HEAD_PAIR), lambda b, h, i: (b, h)),
                  pl.BlockSpec((META_PAD, pair), lambda b, h, i: (0, h)),
                  pl.BlockSpec((META_PAD, HEAD_PAIR), lambda b, h, i: (0, h))],
        out_specs=pl.BlockSpec((tq, HEAD_PAIR), lambda b, h, i: (b * nq + i, h)),
        out_shape=jax.ShapeDtypeStruct(vm.shape, BF16),
        compiler_params=pltpu.CompilerParams(
            dimension_semantics=("parallel", "parallel", "arbitrary"),
            vmem_limit_bytes=VMEM_LIMIT),
        name="mla_attention",
    )(qm, km, vm, mk, mv)


def _merge_ffn_kernel(h1_ref, ysb_ref, ymla_ref, mix_g, w_gate, b_gate, w_sbo, w_mlao, w_out,
                      mixp_g, pre_g, wg, wu, wd, post_g, o_ref):
    h1 = h1_ref[...]
    u = _bf(_rms(h1, mix_g[...]))
    gate = jax.nn.sigmoid(_dot(u, w_gate[...]) + b_gate[...])
    y_sb = _dot(ysb_ref[...], w_sbo[...])
    y_mla = _dot(ymla_ref[...], w_mlao[...])
    merged = gate[:, :D_MODEL] * y_sb + gate[:, D_MODEL:] * y_mla
    m = _dot(_bf(merged), w_out[...])
    h2 = h1 + _rms(m, mixp_g[...])
    o_ref[...] = _ffn(h2, pre_g[...], wg, wu, wd, post_g[...])


def _merge_ffn_call(h1, ysb, ymla, weights, tm):
    t = h1.shape[0]
    row = lambda w: pl.BlockSpec((tm, w), lambda i: (i, 0))
    return pl.pallas_call(
        _merge_ffn_kernel,
        grid=(t // tm,),
        in_specs=[row(D_MODEL), row(SB_WIDTH), row(SB_WIDTH)]
                 + [_const_spec(w.shape) for w in weights],
        out_specs=row(D_MODEL),
        out_shape=jax.ShapeDtypeStruct((t, D_MODEL), F32),
        compiler_params=pltpu.CompilerParams(dimension_semantics=("parallel",),
                                             vmem_limit_bytes=VMEM_LIMIT),
        name="merge_ffn2",
    )(h1, ysb, ymla, *weights)


def _rope_tables(pos):
    half = MLA_ROPE // 2
    inv = ROPE_THETA ** (-jnp.arange(half, dtype=F32) / half)
    ang = pos.astype(F32)[:, None] * inv[None, :]
    cos, sin = jnp.cos(ang), jnp.sin(ang)
    n = pos.shape[0]
    pad = jnp.zeros((n, MLA_QK_PAD - MLA_NOPE - MLA_ROPE), F32)
    cs = jnp.concatenate([jnp.ones((n, MLA_NOPE), F32), cos, cos, pad], axis=1)
    sn = jnp.concatenate([jnp.zeros((n, MLA_NOPE), F32), sin, sin, pad], axis=1)
    return cs, sn


def _rotate_half_cols(w):
    half = MLA_ROPE // 2
    return jnp.concatenate([-w[..., half:], w[..., :half]], axis=-1)


def _mixer_in_weights(w_in, w_uq, w_ukv):
    d = w_in.shape[0]
    w_sbq = w_in[:, 0:512] * (64 ** -0.5)
    w_kr = w_in[:, 2176:2208]
    z_nope = jnp.zeros((d, MLA_NOPE), F32)
    z_pad = jnp.zeros((d, MLA_QK_PAD - MLA_NOPE - MLA_ROPE), F32)
    w_a = jnp.concatenate([w_sbq, w_in[:, 512:2176],
                           z_nope, w_kr, z_pad,
                           z_nope, _rotate_half_cols(w_kr), z_pad], axis=1)
    q_nope, q_rope = w_uq[:, :, :MLA_NOPE], w_uq[:, :, MLA_NOPE:]
    zq_nope = jnp.zeros_like(q_nope)
    zq_pad = jnp.zeros(q_rope.shape[:2] + (MLA_QK_PAD - MLA_NOPE - MLA_ROPE,), F32)
    w_q = jnp.concatenate([
        jnp.concatenate([q_nope, q_rope, zq_pad], -1).reshape(MLA_Q_LORA, -1),
        jnp.concatenate([zq_nope, _rotate_half_cols(q_rope), zq_pad], -1).reshape(MLA_Q_LORA, -1),
    ], axis=1)
    k_nope, v = w_ukv[:, :, :MLA_NOPE], w_ukv[:, :, MLA_NOPE:]
    w_kv = jnp.concatenate([
        jnp.concatenate([k_nope, jnp.zeros_like(k_nope)], -1).reshape(MLA_KV_LORA, -1),
        v.reshape(MLA_KV_LORA, -1)], axis=1)
    return _bf(w_a), _bf(w_q), _bf(w_kv)


def _pad_rows(a, rows):
    return jnp.pad(a, ((0, rows - a.shape[0]), (0, 0)))


def _layer(h, meta, p, tm, tq):
    bsz, seq, d = h.shape
    g = lambda v: v.reshape(1, -1)
    w_a, w_q, w_kv = _mixer_in_weights(p["w_in"], p["w_uq"], p["w_ukv"])
    w1 = [g(p["ffn1_pre_g"]), _bf(p["ffn1_w_gate"]), _bf(p["ffn1_w_up"]), _bf(p["ffn1_w_down"]),
          g(p["ffn1_post_g"]), g(p["mix_pre_g"]), w_a, g(p["q_norm_g"]), w_q,
          g(p["kv_norm_g"]), w_kv]
    cs, sn = _rope_tables(N_META + jnp.arange(seq))
    cs_m, sn_m = _rope_tables(jnp.arange(N_META))

    h1, sbq, sbk, sbv, qm, km, vm = _ffn_proj_call(h.reshape(bsz * seq, d), cs, sn, w1, tm)
    _, _, sbk_m, sbv_m, _, km_m, vm_m = _ffn_proj_call(meta, cs_m, sn_m, w1, N_META)

    ysb = _sb_call(sbq, sbk, sbv, _pad_rows(sbk_m, META_PAD), _pad_rows(sbv_m, META_PAD),
                   bsz, seq, tq)
    ymla = _mla_call(qm, km, vm, _pad_rows(km_m, META_PAD), _pad_rows(vm_m, META_PAD),
                     bsz, seq, tq)

    w4 = [g(p["mix_pre_g"]), _bf(p["w_in"][:, 2208:]), g(p["b_gate"]), _bf(p["w_sb_o"]),
          _bf(p["w_mla_o"]), _bf(p["w_out"]), g(p["mix_post_g"]), g(p["ffn2_pre_g"]),
          _bf(p["ffn2_w_gate"]), _bf(p["ffn2_w_up"]), _bf(p["ffn2_w_down"]),
          g(p["ffn2_post_g"])]
    out = _merge_ffn_call(h1, ysb, ymla, w4, tm)
    return out.reshape(bsz, seq, d)


def kernel(x, meta_tokens, ffn1_pre_g, ffn1_w_gate, ffn1_w_up, ffn1_w_down, ffn1_post_g, mix_pre_g, w_in, b_gate, q_norm_g, w_uq, kv_norm_g, w_ukv, w_sb_o, w_mla_o, w_out, mix_post_g, ffn2_pre_g, ffn2_w_gate, ffn2_w_up, ffn2_w_down, ffn2_post_g):
    names = ("ffn1_pre_g", "ffn1_w_gate", "ffn1_w_up", "ffn1_w_down", "ffn1_post_g", "mix_pre_g",
             "w_in", "b_gate", "q_norm_g", "w_uq", "kv_norm_g", "w_ukv", "w_sb_o", "w_mla_o",
             "w_out", "mix_post_g", "ffn2_pre_g", "ffn2_w_gate", "ffn2_w_up", "ffn2_w_down",
             "ffn2_post_g")
    stacked = (ffn1_pre_g, ffn1_w_gate, ffn1_w_up, ffn1_w_down, ffn1_post_g, mix_pre_g, w_in,
               b_gate, q_norm_g, w_uq, kv_norm_g, w_ukv, w_sb_o, w_mla_o, w_out, mix_post_g,
               ffn2_pre_g, ffn2_w_gate, ffn2_w_up, ffn2_w_down, ffn2_post_g)
    assert all(w.shape[0] == 1 for w in stacked), "single-layer block"
    seq = x.shape[1]
    tq = 256
    tm = 256
    assert seq % tq == 0 and (x.shape[0] * seq) % tm == 0 and seq % tm == 0
    p = {n: w[0] for n, w in zip(names, stacked)}
    return _layer(x, meta_tokens.astype(x.dtype), p, tm, tq)
```

```python
import functools

import jax
import jax.numpy as jnp
from jax import lax
from jax.experimental import pallas as pl
from jax.experimental.pallas import tpu as pltpu

D_MODEL = 1024
D_FF = 2816
N_META = 16
CHUNK = 64
SB_WIDTH = 512
MLA_HEADS = 8
MLA_Q_LORA = 384
MLA_KV_LORA = 256
MLA_NOPE = 64
MLA_ROPE = 32
MLA_V = 64
ROPE_THETA = 10000.0
EPS = 1e-6
NEG_INF = -1e30
HEAD_PAIR = 128
MLA_QK_PAD = 128
META_PAD = 128
SB_TK = 256
VMEM_LIMIT = 56 * 1024 * 1024

F32 = jnp.float32
BF16 = jnp.bfloat16


def _bf(x):
    return x.astype(BF16)


def _dot(a, b):
    return jnp.dot(a, b, preferred_element_type=F32)


def _dot_nt(a, b):
    return lax.dot_general(a, b, (((1,), (1,)), ((), ())), preferred_element_type=F32)


def _rms(x, g):
    return x * lax.rsqrt(jnp.mean(x * x, axis=-1, keepdims=True) + EPS) * g


def _ffn(h, pre_g, wg_ref, wu_ref, wd_ref, post_g):
    xn = _bf(_rms(h, pre_g))
    g = _dot(xn, wg_ref[...])
    u = _dot(xn, wu_ref[...])
    a = _bf(g * jax.nn.sigmoid(g) * u)
    f = _dot(a, wd_ref[...])
    return h + 0.5 * _rms(f, post_g)


def _ffn_proj_kernel(x_ref, cs_ref, sn_ref, pre_g, wg, wu, wd, post_g, mix_g, w_a, qn_g, w_q,
                     kvn_g, w_kv, h1_ref, sbq_ref, sbk_ref, sbv_ref, qm_ref, km_ref, vm_ref):
    h1 = _ffn(x_ref[...], pre_g[...], wg, wu, wd, post_g[...])
    h1_ref[...] = h1
    u = _bf(_rms(h1, mix_g[...]))
    p = _dot(u, w_a[...])
    sbq_ref[...] = _bf(p[:, 0:512])
    sbk_ref[...] = _bf(p[:, 512:1024])
    sbv_ref[...] = _bf(p[:, 1024:1536])
    c_q = p[:, 1536:1920]
    c_kv = p[:, 1920:2176]
    kr = p[:, 2176:2304]
    kr_rot = p[:, 2304:2432]
    cs = cs_ref[...]
    sn = sn_ref[...]
    scale = (MLA_NOPE + MLA_ROPE) ** -0.5

    q2 = _dot(_bf(_rms(c_q, qn_g[...])), w_q[...])
    half = MLA_HEADS * MLA_QK_PAD
    for h in range(MLA_HEADS):
        lo, hi = h * MLA_QK_PAD, (h + 1) * MLA_QK_PAD
        qm_ref[:, lo:hi] = _bf((q2[:, lo:hi] * cs + q2[:, half + lo:half + hi] * sn) * scale)

    kv = _dot(_bf(_rms(c_kv, kvn_g[...])), w_kv[...])
    k_rope = kr * cs + kr_rot * sn
    for h in range(MLA_HEADS):
        lo, hi = h * MLA_QK_PAD, (h + 1) * MLA_QK_PAD
        km_ref[:, lo:hi] = _bf(kv[:, lo:hi] + k_rope)
    vm_ref[...] = _bf(kv[:, half:])


def _const_spec(shape):
    return pl.BlockSpec(shape, lambda *_: (0,) * len(shape), pipeline_mode=pl.Buffered(1))


def _ffn_proj_call(x2d, cs, sn, weights, tm):
    t = x2d.shape[0]
    n_pos_blocks = cs.shape[0] // tm
    row = lambda w: pl.BlockSpec((tm, w), lambda i: (i, 0))
    pos = pl.BlockSpec((tm, MLA_QK_PAD), lambda i: (i % n_pos_blocks, 0))
    in_specs = [row(D_MODEL), pos, pos] + [_const_spec(w.shape) for w in weights]
    widths = (D_MODEL, SB_WIDTH, SB_WIDTH, SB_WIDTH, MLA_HEADS * MLA_QK_PAD,
              MLA_HEADS * MLA_QK_PAD, MLA_HEADS * MLA_V)
    dtypes = (F32,) + (BF16,) * 6
    return pl.pallas_call(
        _ffn_proj_kernel,
        grid=(t // tm,),
        in_specs=in_specs,
        out_specs=[row(w) for w in widths],
        out_shape=[jax.ShapeDtypeStruct((t, w), d) for w, d in zip(widths, dtypes)],
        compiler_params=pltpu.CompilerParams(dimension_semantics=("parallel",),
                                             vmem_limit_bytes=VMEM_LIMIT),
        name="ffn1_proj",
    )(x2d, cs, sn, *weights)


def _sb_tile(qh, kblk, vblk, ntri2, mask):
    z = _dot_nt(qh, kblk)
    sp = jnp.maximum(z, 0.0) + jnp.log(1.0 + jnp.exp(-jnp.abs(z)))
    log_b = z - sp
    if mask is not None:
        sp = jnp.where(mask, sp, 0.0)
    hi = _bf(sp)
    lo = _bf(sp - hi.astype(F32))
    a = jnp.exp(log_b + _dot(jnp.concatenate([hi, lo], axis=1), ntri2))
    if mask is not None:
        a = jnp.where(mask, a, 0.0)
    return _dot(_bf(a), vblk), jnp.sum(sp, axis=-1, keepdims=True)


def _sb_kernel(q_ref, k_ref, v_ref, mk_ref, mv_ref, ntri_ref, o_ref, *, tq):
    tk = SB_TK
    assert tq == 2 * tk
    qi = pl.program_id(2)
    start = pl.multiple_of(qi * tq, tq)
    lane = lax.broadcasted_iota(jnp.int32, (tq, HEAD_PAIR), 1)
    row = lax.broadcasted_iota(jnp.int32, (tq, tk), 0)
    col = lax.broadcasted_iota(jnp.int32, (tq, tk), 1)
    near_mask = (col < row)[:tk]
    far_mask = (row >= tk) | (col < row)
    meta_mask = lax.broadcasted_iota(jnp.int32, (tq, META_PAD), 1) < N_META
    q = q_ref[...]
    ntri = ntri_ref[...]
    ntri_meta = jnp.concatenate([ntri[:META_PAD, :META_PAD], ntri[tk:tk + META_PAD, :META_PAD]],
                                axis=0)
    zero = jnp.zeros_like(q)
    qs = [jnp.where(lane < 64, q, zero), jnp.where(lane >= 64, q, zero)]

    def kv(s):
        return k_ref[pl.ds(s, tk), :], v_ref[pl.ds(s, tk), :]

    def add(state, pv, sp_sum):
        acc, dist = state
        return acc + jnp.exp(-dist) * pv, dist + sp_sum

    states = []
    k_near, v_near = kv(start + tk)
    k_far, v_far = kv(start)
    for qh in qs:
        pv, sp_sum = _sb_tile(qh[tk:], k_near, v_near, ntri, near_mask)
        state = (jnp.concatenate([jnp.zeros((tk, HEAD_PAIR), F32), pv], axis=0),
                 jnp.concatenate([jnp.zeros((tk, 1), F32), sp_sum], axis=0))
        states.append(add(state, *_sb_tile(qh, k_far, v_far, ntri, far_mask)))

    def body(j, carry):
        base = pl.multiple_of(start - (j + 1) * tq, tq)
        out = []
        for qh, state in zip(qs, carry):
            for s in (base + tk, base):
                state = add(state, *_sb_tile(qh, *kv(s), ntri, None))
            out.append(state)
        return tuple(out)

    states = lax.fori_loop(0, qi, body, tuple(states))
    accs = [add(state, *_sb_tile(qh, mk_ref[...], mv_ref[...], ntri_meta, meta_mask))[0]
            for qh, state in zip(qs, states)]
    o_ref[...] = _bf(jnp.where(lane < 64, accs[0], accs[1]))


def _sb_call(sbq, sbk, sbv, mk, mv, bsz, seq, tq):
    nq = seq // tq
    ntri = -(lax.broadcasted_iota(jnp.int32, (SB_TK, SB_TK), 0)
             > lax.broadcasted_iota(jnp.int32, (SB_TK, SB_TK), 1)).astype(BF16)
    ntri2 = jnp.concatenate([ntri, ntri], axis=0)
    qspec = pl.BlockSpec((tq, HEAD_PAIR), lambda b, h, i: (b * nq + i, h))
    kvspec = pl.BlockSpec((seq, HEAD_PAIR), lambda b, h, i: (b, h))
    mspec = pl.BlockSpec((META_PAD, HEAD_PAIR), lambda b, h, i: (0, h))
    return pl.pallas_call(
        functools.partial(_sb_kernel, tq=tq),
        grid=(bsz, SB_WIDTH // HEAD_PAIR, nq),
        in_specs=[qspec, kvspec, kvspec, mspec, mspec,
                  pl.BlockSpec((2 * SB_TK, SB_TK), lambda b, h, i: (0, 0))],
        out_specs=qspec,
        out_shape=jax.ShapeDtypeStruct(sbq.shape, BF16),
        compiler_params=pltpu.CompilerParams(
            dimension_semantics=("parallel", "parallel", "arbitrary"),
            vmem_limit_bytes=VMEM_LIMIT),
        name="sb_attention",
    )(sbq, sbk, sbv, mk, mv, ntri2)


def _softmax_step(q, kblk, vblk, state, mask):
    m, l, acc = state
    s = _dot_nt(q, kblk)
    if mask is not None:
        s = jnp.where(mask, s, NEG_INF)
    m_new = jnp.maximum(m, jnp.max(s, axis=-1, keepdims=True))
    alpha = jnp.exp(m - m_new)
    p = jnp.exp(s - m_new)
    return (m_new, alpha * l + jnp.sum(p, axis=-1, keepdims=True),
            alpha * acc + _dot(_bf(p), vblk))


def _mla_kernel(q_ref, k_ref, v_ref, mk_ref, mv_ref, o_ref, *, tq):
    half = tq // 2
    qi = pl.program_id(2)
    start = pl.multiple_of(qi * tq, tq)
    lane = lax.broadcasted_iota(jnp.int32, (tq, HEAD_PAIR), 1)
    row = lax.broadcasted_iota(jnp.int32, (half, tq), 0)
    col = lax.broadcasted_iota(jnp.int32, (half, tq), 1)
    top_mask = ((col // CHUNK) <= (row // CHUNK))[:, :half]
    bot_mask = (col // CHUNK) <= ((row + half) // CHUNK)
    meta_mask = lax.broadcasted_iota(jnp.int32, (tq, META_PAD), 1) < N_META
    lanes = [(hh * MLA_QK_PAD, (hh + 1) * MLA_QK_PAD) for hh in range(2)]
    qs = [q_ref[:, lo:hi] for lo, hi in lanes]

    init = (jnp.full((tq, 1), NEG_INF, F32), jnp.zeros((tq, 1), F32),
            jnp.zeros((tq, HEAD_PAIR), F32))
    states = tuple(_softmax_step(q, mk_ref[:, lo:hi], mv_ref[...], init, meta_mask)
                   for q, (lo, hi) in zip(qs, lanes))

    def body(j, carry):
        s = pl.multiple_of(j * tq, tq)
        return tuple(_softmax_step(q, k_ref[pl.ds(s, tq), lo:hi], v_ref[pl.ds(s, tq), :],
                                   state, None)
                     for q, (lo, hi), state in zip(qs, lanes, carry))

    states = lax.fori_loop(0, qi, body, states)

    outs = []
    for q, (lo, hi), state in zip(qs, lanes, states):
        top = _softmax_step(q[:half], k_ref[pl.ds(start, half), lo:hi],
                            v_ref[pl.ds(start, half), :], tuple(x[:half] for x in state),
                            top_mask)
        bot = _softmax_step(q[half:], k_ref[pl.ds(start, tq), lo:hi],
                            v_ref[pl.ds(start, tq), :], tuple(x[half:] for x in state),
                            bot_mask)
        outs.append(jnp.concatenate([top[2] / top[1], bot[2] / bot[1]], axis=0))
    o_ref[...] = _bf(jnp.where(lane < 64, outs[0], outs[1]))


def _mla_call(qm, km, vm, mk, mv, bsz, seq, tq):
    nq = seq // tq
    pair = 2 * MLA_QK_PAD
    return pl.pallas_call(
        functools.partial(_mla_kernel, tq=tq),
        grid=(bsz, MLA_HEADS // 2, nq),
        in_specs=[pl.BlockSpec((tq, pair), lambda b, h, i: (b * nq + i, h)),
                  pl.BlockSpec((seq, pair), lambda b, h, i: (b, h)),
                  pl.BlockSpec((seq, HEAD_PAIR), lambda b, h, i: (b, h)),
                  pl.BlockSpec((META_PAD, pair), lambda b, h, i: (0, h)),
                  pl.BlockSpec((META_PAD, HEAD_PAIR), lambda b, h, i: (0, h))],
        out_specs=pl.BlockSpec((tq, HEAD_PAIR), lambda b, h, i: (b * nq + i, h)),
        out_shape=jax.ShapeDtypeStruct(vm.shape, BF16),
        compiler_params=pltpu.CompilerParams(
            dimension_semantics=("parallel", "parallel", "arbitrary"),
            vmem_limit_bytes=VMEM_LIMIT),
        name="mla_attention",
    )(qm, km, vm, mk, mv)


def _merge_ffn_kernel(h1_ref, ysb_ref, ymla_ref, mix_g, w_gate, b_gate, w_sbo, w_mlao, w_out,
                      mixp_g, pre_g, wg, wu, wd, post_g, o_ref):
    h1 = h1_ref[...]
    u = _bf(_rms(h1, mix_g[...]))
    gate = jax.nn.sigmoid(_dot(u, w_gate[...]) + b_gate[...])
    y_sb = _dot(ysb_ref[...], w_sbo[...])
    y_mla = _dot(ymla_ref[...], w_mlao[...])
    merged = gate[:, :D_MODEL] * y_sb + gate[:, D_MODEL:] * y_mla
    m = _dot(_bf(merged), w_out[...])
    h2 = h1 + _rms(m, mixp_g[...])
    o_ref[...] = _ffn(h2, pre_g[...], wg, wu, wd, post_g[...])


def _merge_ffn_call(h1, ysb, ymla, weights, tm):
    t = h1.shape[0]
    row = lambda w: pl.BlockSpec((tm, w), lambda i: (i, 0))
    return pl.pallas_call(
        _merge_ffn_kernel,
        grid=(t // tm,),
        in_specs=[row(D_MODEL), row(SB_WIDTH), row(SB_WIDTH)]
                 + [_const_spec(w.shape) for w in weights],
        out_specs=row(D_MODEL),
        out_shape=jax.ShapeDtypeStruct((t, D_MODEL), F32),
        compiler_params=pltpu.CompilerParams(dimension_semantics=("parallel",),
                                             vmem_limit_bytes=VMEM_LIMIT),
        name="merge_ffn2",
    )(h1, ysb, ymla, *weights)


def _rope_tables(pos):
    half = MLA_ROPE // 2
    inv = ROPE_THETA ** (-jnp.arange(half, dtype=F32) / half)
    ang = pos.astype(F32)[:, None] * inv[None, :]
    cos, sin = jnp.cos(ang), jnp.sin(ang)
    n = pos.shape[0]
    pad = jnp.zeros((n, MLA_QK_PAD - MLA_NOPE - MLA_ROPE), F32)
    cs = jnp.concatenate([jnp.ones((n, MLA_NOPE), F32), cos, cos, pad], axis=1)
    sn = jnp.concatenate([jnp.zeros((n, MLA_NOPE), F32), sin, sin, pad], axis=1)
    return cs, sn


def _rotate_half_cols(w):
    half = MLA_ROPE // 2
    return jnp.concatenate([-w[..., half:], w[..., :half]], axis=-1)


def _mixer_in_weights(w_in, w_uq, w_ukv):
    d = w_in.shape[0]
    w_sbq = w_in[:, 0:512] * (64 ** -0.5)
    w_kr = w_in[:, 2176:2208]
    z_nope = jnp.zeros((d, MLA_NOPE), F32)
    z_pad = jnp.zeros((d, MLA_QK_PAD - MLA_NOPE - MLA_ROPE), F32)
    w_a = jnp.concatenate([w_sbq, w_in[:, 512:2176],
                           z_nope, w_kr, z_pad,
                           z_nope, _rotate_half_cols(w_kr), z_pad], axis=1)
    q_nope, q_rope = w_uq[:, :, :MLA_NOPE], w_uq[:, :, MLA_NOPE:]
    zq_nope = jnp.zeros_like(q_nope)
    zq_pad = jnp.zeros(q_rope.shape[:2] + (MLA_QK_PAD - MLA_NOPE - MLA_ROPE,), F32)
    w_q = jnp.concatenate([
        jnp.concatenate([q_nope, q_rope, zq_pad], -1).reshape(MLA_Q_LORA, -1),
        jnp.concatenate([zq_nope, _rotate_half_cols(q_rope), zq_pad], -1).reshape(MLA_Q_LORA, -1),
    ], axis=1)
    k_nope, v = w_ukv[:, :, :MLA_NOPE], w_ukv[:, :, MLA_NOPE:]
    w_kv = jnp.concatenate([
        jnp.concatenate([k_nope, jnp.zeros_like(k_nope)], -1).reshape(MLA_KV_LORA, -1),
        v.reshape(MLA_KV_LORA, -1)], axis=1)
    return _bf(w_a), _bf(w_q), _bf(w_kv)


def _pad_rows(a, rows):
    return jnp.pad(a, ((0, rows - a.shape[0]), (0, 0)))


def _layer(h, meta, p, tm, tq):
    bsz, seq, d = h.shape
    g = lambda v: v.reshape(1, -1)
    w_a, w_q, w_kv = _mixer_in_weights(p["w_in"], p["w_uq"], p["w_ukv"])
    w1 = [g(p["ffn1_pre_g"]), _bf(p["ffn1_w_gate"]), _bf(p["ffn1_w_up"]), _bf(p["ffn1_w_down"]),
          g(p["ffn1_post_g"]), g(p["mix_pre_g"]), w_a, g(p["q_norm_g"]), w_q,
          g(p["kv_norm_g"]), w_kv]
    cs, sn = _rope_tables(N_META + jnp.arange(seq))
    cs_m, sn_m = _rope_tables(jnp.arange(N_META))

    h1, sbq, sbk, sbv, qm, km, vm = _ffn_proj_call(h.reshape(bsz * seq, d), cs, sn, w1, tm)
    _, _, sbk_m, sbv_m, _, km_m, vm_m = _ffn_proj_call(meta, cs_m, sn_m, w1, N_META)

    ysb = _sb_call(sbq, sbk, sbv, _pad_rows(sbk_m, META_PAD), _pad_rows(sbv_m, META_PAD),
                   bsz, seq, tq)
    ymla = _mla_call(qm, km, vm, _pad_rows(km_m, META_PAD), _pad_rows(vm_m, META_PAD),
                     bsz, seq, tq)

    w4 = [g(p["mix_pre_g"]), _bf(p["w_in"][:, 2208:]), g(p["b_gate"]), _bf(p["w_sb_o"]),
          _bf(p["w_mla_o"]), _bf(p["w_out"]), g(p["mix_post_g"]), g(p["ffn2_pre_g"]),
          _bf(p["ffn2_w_gate"]), _bf(p["ffn2_w_up"]), _bf(p["ffn2_w_down"]),
          g(p["ffn2_post_g"])]
    out = _merge_ffn_call(h1, ysb, ymla, w4, tm)
    return out.reshape(bsz, seq, d)


def kernel(x, meta_tokens, ffn1_pre_g, ffn1_w_gate, ffn1_w_up, ffn1_w_down, ffn1_post_g, mix_pre_g, w_in, b_gate, q_norm_g, w_uq, kv_norm_g, w_ukv, w_sb_o, w_mla_o, w_out, mix_post_g, ffn2_pre_g, ffn2_w_gate, ffn2_w_up, ffn2_w_down, ffn2_post_g):
    names = ("ffn1_pre_g", "ffn1_w_gate", "ffn1_w_up", "ffn1_w_down", "ffn1_post_g", "mix_pre_g",
             "w_in", "b_gate", "q_norm_g", "w_uq", "kv_norm_g", "w_ukv", "w_sb_o", "w_mla_o",
             "w_out", "mix_post_g", "ffn2_pre_g", "ffn2_w_gate", "ffn2_w_up", "ffn2_w_down",
             "ffn2_post_g")
    stacked = (ffn1_pre_g, ffn1_w_gate, ffn1_w_up, ffn1_w_down, ffn1_post_g, mix_pre_g, w_in,
               b_gate, q_norm_g, w_uq, kv_norm_g, w_ukv, w_sb_o, w_mla_o, w_out, mix_post_g,
               ffn2_pre_g, ffn2_w_gate, ffn2_w_up, ffn2_w_down, ffn2_post_g)
    assert all(w.shape[0] == 1 for w in stacked), "single-layer block"
    seq = x.shape[1]
    tq = 2 * SB_TK
    tm = 256
    assert seq % tq == 0 and (x.shape[0] * seq) % tm == 0 and seq % tm == 0
    p = {n: w[0] for n, w in zip(names, stacked)}
    return _layer(x, meta_tokens.astype(x.dtype), p, tm, tq)
```

```python
import functools

import jax
import jax.numpy as jnp
from jax import lax
from jax.experimental import pallas as pl
from jax.experimental.pallas import tpu as pltpu

D_MODEL = 1024
D_FF = 2816
N_META = 16
CHUNK = 64
SB_WIDTH = 512
MLA_HEADS = 8
MLA_Q_LORA = 384
MLA_KV_LORA = 256
MLA_NOPE = 64
MLA_ROPE = 32
MLA_V = 64
ROPE_THETA = 10000.0
EPS = 1e-6
NEG_INF = -1e30
LOG2_E = 1.4426950408889634
HEAD_PAIR = 128
MLA_QK_PAD = 128
META_PAD = 128
SB_TK = 256
VMEM_LIMIT = 56 * 1024 * 1024

F32 = jnp.float32
BF16 = jnp.bfloat16


def _bf(x):
    return x.astype(BF16)


def _dot(a, b):
    return jnp.dot(a, b, preferred_element_type=F32)


def _dot_nt(a, b):
    return lax.dot_general(a, b, (((1,), (1,)), ((), ())), preferred_element_type=F32)


def _rms(x, g):
    return x * lax.rsqrt(jnp.mean(x * x, axis=-1, keepdims=True) + EPS) * g


def _row_stat(x):
    return jnp.broadcast_to(x, (x.shape[0], HEAD_PAIR))


def _across_lanes(stat, width):
    return stat if width == HEAD_PAIR else jnp.concatenate([stat] * (width // HEAD_PAIR), axis=1)


def _ffn(h, pre_g, wg_ref, wu_ref, wd_ref, post_g):
    xn = _bf(_rms(h, pre_g))
    g = _dot(xn, wg_ref[...])
    u = _dot(xn, wu_ref[...])
    a = _bf(g * jax.nn.sigmoid(g) * u)
    f = _dot(a, wd_ref[...])
    return h + 0.5 * _rms(f, post_g)


def _ffn_proj_kernel(x_ref, cs_ref, sn_ref, pre_g, wg, wu, wd, post_g, mix_g, w_a, qn_g, w_q,
                     kvn_g, w_kv, h1_ref, sbq_ref, sbk_ref, sbv_ref, qm_ref, km_ref, vm_ref):
    h1 = _ffn(x_ref[...], pre_g[...], wg, wu, wd, post_g[...])
    h1_ref[...] = h1
    u = _bf(_rms(h1, mix_g[...]))
    p = _dot(u, w_a[...])
    sbq_ref[...] = _bf(p[:, 0:512])
    sbk_ref[...] = _bf(p[:, 512:1024])
    sbv_ref[...] = _bf(p[:, 1024:1536])
    c_q = p[:, 1536:1920]
    c_kv = p[:, 1920:2176]
    kr = p[:, 2176:2304]
    kr_rot = p[:, 2304:2432]
    cs = cs_ref[...]
    sn = sn_ref[...]
    scale = (MLA_NOPE + MLA_ROPE) ** -0.5 * LOG2_E

    q2 = _dot(_bf(_rms(c_q, qn_g[...])), w_q[...])
    half = MLA_HEADS * MLA_QK_PAD
    for h in range(MLA_HEADS):
        lo, hi = h * MLA_QK_PAD, (h + 1) * MLA_QK_PAD
        qm_ref[:, lo:hi] = _bf((q2[:, lo:hi] * cs + q2[:, half + lo:half + hi] * sn) * scale)

    kv = _dot(_bf(_rms(c_kv, kvn_g[...])), w_kv[...])
    k_rope = kr * cs + kr_rot * sn
    for h in range(MLA_HEADS):
        lo, hi = h * MLA_QK_PAD, (h + 1) * MLA_QK_PAD
        km_ref[:, lo:hi] = _bf(kv[:, lo:hi] + k_rope)
    vm_ref[...] = _bf(kv[:, half:])


def _const_spec(shape):
    return pl.BlockSpec(shape, lambda *_: (0,) * len(shape), pipeline_mode=pl.Buffered(1))


def _ffn_proj_call(x2d, cs, sn, weights, tm):
    t = x2d.shape[0]
    n_pos_blocks = cs.shape[0] // tm
    row = lambda w: pl.BlockSpec((tm, w), lambda i: (i, 0))
    pos = pl.BlockSpec((tm, MLA_QK_PAD), lambda i: (i % n_pos_blocks, 0))
    in_specs = [row(D_MODEL), pos, pos] + [_const_spec(w.shape) for w in weights]
    widths = (D_MODEL, SB_WIDTH, SB_WIDTH, SB_WIDTH, MLA_HEADS * MLA_QK_PAD,
              MLA_HEADS * MLA_QK_PAD, MLA_HEADS * MLA_V)
    dtypes = (F32,) + (BF16,) * 6
    return pl.pallas_call(
        _ffn_proj_kernel,
        grid=(t // tm,),
        in_specs=in_specs,
        out_specs=[row(w) for w in widths],
        out_shape=[jax.ShapeDtypeStruct((t, w), d) for w, d in zip(widths, dtypes)],
        compiler_params=pltpu.CompilerParams(dimension_semantics=("parallel",),
                                             vmem_limit_bytes=VMEM_LIMIT),
        name="ffn1_proj",
    )(x2d, cs, sn, *weights)


def _sb_tile(qh, kblk, vblk, ntri, mask):
    z2 = _dot_nt(qh, kblk)
    neg_abs = lax.bitcast_convert_type(
        lax.bitcast_convert_type(z2, jnp.uint32) | jnp.uint32(0x80000000), F32)
    sp = jnp.maximum(z2, 0.0) + jnp.log2(1.0 + jnp.exp2(neg_abs))
    log2_b = z2 - sp
    if mask is not None:
        sp = jnp.where(mask, sp, 0.0)
    a = jnp.exp2(log2_b + _dot(_bf(sp), ntri))
    if mask is not None:
        a = jnp.where(mask, a, 0.0)
    return _dot(_bf(a), vblk), _row_stat(jnp.sum(sp, axis=-1, keepdims=True))


def _sb_kernel(q_ref, k_ref, v_ref, mk_ref, mv_ref, ntri_ref, o_ref, meta_ref, *, tq):
    tk = SB_TK
    assert tq == 2 * tk
    qi = pl.program_id(2)
    start = pl.multiple_of(qi * tq, tq)
    lane = lax.broadcasted_iota(jnp.int32, (tq, HEAD_PAIR), 1)
    row = lax.broadcasted_iota(jnp.int32, (tq, tk), 0)
    col = lax.broadcasted_iota(jnp.int32, (tq, tk), 1)
    far_mask = (row >= tk) | (col < row)
    near_mask = lax.broadcasted_iota(jnp.int32, (tk, tk), 1) < lax.broadcasted_iota(
        jnp.int32, (tk, tk), 0)
    meta_mask = lax.broadcasted_iota(jnp.int32, (tq, META_PAD), 1) < N_META
    q = q_ref[...]
    ntri = ntri_ref[...]
    zero = jnp.zeros_like(q)
    qs = [jnp.where(lane < 64, q, zero), jnp.where(lane >= 64, q, zero)]

    def kv(s):
        return k_ref[pl.ds(s, tk), :], v_ref[pl.ds(s, tk), :]

    def add(state, pv, sp_sum):
        acc, log2_gap = state
        return acc + jnp.exp2(log2_gap) * pv, log2_gap - sp_sum

    ntri_meta = ntri[:META_PAD, :META_PAD]
    for hh, qh in enumerate(qs):
        meta_ref[hh] = _sb_tile(qh, mk_ref[...], mv_ref[...], ntri_meta, meta_mask)[0]

    states = []
    k_near, v_near = kv(start + tk)
    k_far, v_far = kv(start)
    for qh in qs:
        pv, sp_sum = _sb_tile(qh[tk:], k_near, v_near, ntri, near_mask)
        top = jnp.zeros((tk, HEAD_PAIR), F32)
        state = (jnp.concatenate([top, pv], axis=0), jnp.concatenate([top, -sp_sum], axis=0))
        states.append(add(state, *_sb_tile(qh, k_far, v_far, ntri, far_mask)))

    def body(j, carry):
        base = pl.multiple_of(start - (j + 1) * tq, tq)
        out = []
        for qh, state in zip(qs, carry):
            for s in (base + tk, base):
                state = add(state, *_sb_tile(qh, *kv(s), ntri, None))
            out.append(state)
        return tuple(out)

    states = lax.fori_loop(0, qi, body, tuple(states))
    accs = [acc + jnp.exp2(log2_gap) * meta_ref[hh] for hh, (acc, log2_gap) in enumerate(states)]
    o_ref[...] = _bf(jnp.where(lane < 64, accs[0], accs[1]))


def _sb_call(sbq, sbk, sbv, mk, mv, bsz, seq, tq):
    nq = seq // tq
    ntri = -(lax.broadcasted_iota(jnp.int32, (SB_TK, SB_TK), 0)
             > lax.broadcasted_iota(jnp.int32, (SB_TK, SB_TK), 1)).astype(BF16)
    qspec = pl.BlockSpec((tq, HEAD_PAIR), lambda b, h, i: (b * nq + i, h))
    kvspec = pl.BlockSpec((seq, HEAD_PAIR), lambda b, h, i: (b, h))
    mspec = pl.BlockSpec((META_PAD, HEAD_PAIR), lambda b, h, i: (0, h))
    return pl.pallas_call(
        functools.partial(_sb_kernel, tq=tq),
        grid=(bsz, SB_WIDTH // HEAD_PAIR, nq),
        in_specs=[qspec, kvspec, kvspec, mspec, mspec,
                  pl.BlockSpec((SB_TK, SB_TK), lambda b, h, i: (0, 0))],
        out_specs=qspec,
        out_shape=jax.ShapeDtypeStruct(sbq.shape, BF16),
        scratch_shapes=[pltpu.VMEM((2, tq, HEAD_PAIR), F32)],
        compiler_params=pltpu.CompilerParams(
            dimension_semantics=("parallel", "parallel", "arbitrary"),
            vmem_limit_bytes=VMEM_LIMIT),
        name="sb_attention",
    )(sbq, sbk, sbv, mk, mv, ntri)


def _softmax_step(q, blocks, state):
    scores = []
    for kblk, _, mask in blocks:
        s = _dot_nt(q, kblk)
        scores.append(s if mask is None else jnp.where(mask, s, NEG_INF))
    m_new = functools.reduce(jnp.maximum, [_row_stat(jnp.max(s, axis=-1, keepdims=True))
                                           for s in scores])
    if state is not None:
        m, l, acc = state
        m_new = jnp.maximum(m, m_new)
        alpha = jnp.exp2(m - m_new)
    probs = [jnp.exp2(s - _across_lanes(m_new, s.shape[1])) for s in scores]
    l_terms = [_row_stat(jnp.sum(p, axis=-1, keepdims=True)) for p in probs]
    pv_terms = [_dot(_bf(p), vblk) for p, (_, vblk, _) in zip(probs, blocks)]
    if state is not None:
        l_terms.insert(0, alpha * l)
        pv_terms.insert(0, alpha * acc)
    return (m_new, functools.reduce(lambda a, b: a + b, l_terms),
            functools.reduce(lambda a, b: a + b, pv_terms))


def _mla_kernel(q_ref, k_ref, v_ref, mk_ref, mv_ref, o_ref, *, tq):
    half = tq // 2
    qi = pl.program_id(2)
    start = pl.multiple_of(qi * tq, tq)
    lane = lax.broadcasted_iota(jnp.int32, (tq, HEAD_PAIR), 1)
    row = lax.broadcasted_iota(jnp.int32, (half, tq), 0)
    col = lax.broadcasted_iota(jnp.int32, (half, tq), 1)
    top_mask = ((col // CHUNK) <= (row // CHUNK))[:, :half]
    bot_mask = (col // CHUNK) <= ((row + half) // CHUNK)
    meta_mask = lax.broadcasted_iota(jnp.int32, (half, META_PAD), 1) < N_META
    lanes = [(hh * MLA_QK_PAD, (hh + 1) * MLA_QK_PAD) for hh in range(2)]
    qs = [q_ref[:, lo:hi] for lo, hi in lanes]

    states = []
    for q, (lo, hi) in zip(qs, lanes):
        meta = (mk_ref[:, lo:hi], mv_ref[...], meta_mask)
        top = _softmax_step(q[:half], [meta, (k_ref[pl.ds(start, half), lo:hi],
                                              v_ref[pl.ds(start, half), :], top_mask)], None)
        bot = _softmax_step(q[half:], [meta, (k_ref[pl.ds(start, tq), lo:hi],
                                              v_ref[pl.ds(start, tq), :], bot_mask)], None)
        states.append(tuple(jnp.concatenate([t, b], axis=0) for t, b in zip(top, bot)))

    def body(j, carry):
        s = pl.multiple_of(j * tq, tq)
        return tuple(_softmax_step(q, [(k_ref[pl.ds(s, tq), lo:hi], v_ref[pl.ds(s, tq), :],
                                        None)], state)
                     for q, (lo, hi), state in zip(qs, lanes, carry))

    states = lax.fori_loop(0, qi, body, tuple(states))
    outs = [acc / l for _, l, acc in states]
    o_ref[...] = _bf(jnp.where(lane < 64, outs[0], outs[1]))


def _mla_call(qm, km, vm, mk, mv, bsz, seq, tq):
    nq = seq // tq
    pair = 2 * MLA_QK_PAD
    return pl.pallas_call(
        functools.partial(_mla_kernel, tq=tq),
        grid=(bsz, MLA_HEADS // 2, nq),
        in_specs=[pl.BlockSpec((tq, pair), lambda b, h, i: (b * nq + i, h)),
                  pl.BlockSpec((seq, pair), lambda b, h, i: (b, h)),
                  pl.BlockSpec((seq, HEAD_PAIR), lambda b, h, i: (b, h)),
                  pl.BlockSpec((META_PAD, pair), lambda b, h, i: (0, h)),
                  pl.BlockSpec((META_PAD, HEAD_PAIR), lambda b, h, i: (0, h))],
        out_specs=pl.BlockSpec((tq, HEAD_PAIR), lambda b, h, i: (b * nq + i, h)),
        out_shape=jax.ShapeDtypeStruct(vm.shape, BF16),
        compiler_params=pltpu.CompilerParams(
            dimension_semantics=("parallel", "parallel", "arbitrary"),
            vmem_limit_bytes=VMEM_LIMIT),
        name="mla_attention",
    )(qm, km, vm, mk, mv)


def _merge_ffn_kernel(h1_ref, ysb_ref, ymla_ref, mix_g, w_gate, b_gate, w_sbo, w_mlao, w_out,
                      mixp_g, pre_g, wg, wu, wd, post_g, o_ref):
    h1 = h1_ref[...]
    u = _bf(_rms(h1, mix_g[...]))
    gate = jax.nn.sigmoid(_dot(u, w_gate[...]) + b_gate[...])
    y_sb = _dot(ysb_ref[...], w_sbo[...])
    y_mla = _dot(ymla_ref[...], w_mlao[...])
    merged = gate[:, :D_MODEL] * y_sb + gate[:, D_MODEL:] * y_mla
    m = _dot(_bf(merged), w_out[...])
    h2 = h1 + _rms(m, mixp_g[...])
    o_ref[...] = _ffn(h2, pre_g[...], wg, wu, wd, post_g[...])


def _merge_ffn_call(h1, ysb, ymla, weights, tm):
    t = h1.shape[0]
    row = lambda w: pl.BlockSpec((tm, w), lambda i: (i, 0))
    return pl.pallas_call(
        _merge_ffn_kernel,
        grid=(t // tm,),
        in_specs=[row(D_MODEL), row(SB_WIDTH), row(SB_WIDTH)]
                 + [_const_spec(w.shape) for w in weights],
        out_specs=row(D_MODEL),
        out_shape=jax.ShapeDtypeStruct((t, D_MODEL), F32),
        compiler_params=pltpu.CompilerParams(dimension_semantics=("parallel",),
                                             vmem_limit_bytes=VMEM_LIMIT),
        name="merge_ffn2",
    )(h1, ysb, ymla, *weights)


def _rope_tables(pos):
    half = MLA_ROPE // 2
    inv = ROPE_THETA ** (-jnp.arange(half, dtype=F32) / half)
    ang = pos.astype(F32)[:, None] * inv[None, :]
    cos, sin = jnp.cos(ang), jnp.sin(ang)
    n = pos.shape[0]
    pad = jnp.zeros((n, MLA_QK_PAD - MLA_NOPE - MLA_ROPE), F32)
    cs = jnp.concatenate([jnp.ones((n, MLA_NOPE), F32), cos, cos, pad], axis=1)
    sn = jnp.concatenate([jnp.zeros((n, MLA_NOPE), F32), sin, sin, pad], axis=1)
    return cs, sn


def _rotate_half_cols(w):
    half = MLA_ROPE // 2
    return jnp.concatenate([-w[..., half:], w[..., :half]], axis=-1)


def _mixer_in_weights(w_in, w_uq, w_ukv):
    d = w_in.shape[0]
    w_sbq = w_in[:, 0:512] * (64 ** -0.5 * LOG2_E)
    w_kr = w_in[:, 2176:2208]
    z_nope = jnp.zeros((d, MLA_NOPE), F32)
    z_pad = jnp.zeros((d, MLA_QK_PAD - MLA_NOPE - MLA_ROPE), F32)
    w_a = jnp.concatenate([w_sbq, w_in[:, 512:2176],
                           z_nope, w_kr, z_pad,
                           z_nope, _rotate_half_cols(w_kr), z_pad], axis=1)
    q_nope, q_rope = w_uq[:, :, :MLA_NOPE], w_uq[:, :, MLA_NOPE:]
    zq_nope = jnp.zeros_like(q_nope)
    zq_pad = jnp.zeros(q_rope.shape[:2] + (MLA_QK_PAD - MLA_NOPE - MLA_ROPE,), F32)
    w_q = jnp.concatenate([
        jnp.concatenate([q_nope, q_rope, zq_pad], -1).reshape(MLA_Q_LORA, -1),
        jnp.concatenate([zq_nope, _rotate_half_cols(q_rope), zq_pad], -1).reshape(MLA_Q_LORA, -1),
    ], axis=1)
    k_nope, v = w_ukv[:, :, :MLA_NOPE], w_ukv[:, :, MLA_NOPE:]
    w_kv = jnp.concatenate([
        jnp.concatenate([k_nope, jnp.zeros_like(k_nope)], -1).reshape(MLA_KV_LORA, -1),
        v.reshape(MLA_KV_LORA, -1)], axis=1)
    return _bf(w_a), _bf(w_q), _bf(w_kv)


def _pad_rows(a, rows):
    return jnp.pad(a, ((0, rows - a.shape[0]), (0, 0)))


def _layer(h, meta, p, tm, tq):
    bsz, seq, d = h.shape
    g = lambda v: v.reshape(1, -1)
    w_a, w_q, w_kv = _mixer_in_weights(p["w_in"], p["w_uq"], p["w_ukv"])
    w1 = [g(p["ffn1_pre_g"]), _bf(p["ffn1_w_gate"]), _bf(p["ffn1_w_up"]), _bf(p["ffn1_w_down"]),
          g(p["ffn1_post_g"]), g(p["mix_pre_g"]), w_a, g(p["q_norm_g"]), w_q,
          g(p["kv_norm_g"]), w_kv]
    cs, sn = _rope_tables(N_META + jnp.arange(seq))
    cs_m, sn_m = _rope_tables(jnp.arange(N_META))

    h1, sbq, sbk, sbv, qm, km, vm = _ffn_proj_call(h.reshape(bsz * seq, d), cs, sn, w1, tm)
    _, _, sbk_m, sbv_m, _, km_m, vm_m = _ffn_proj_call(meta, cs_m, sn_m, w1, N_META)

    ysb = _sb_call(sbq, sbk, sbv, _pad_rows(sbk_m, META_PAD), _pad_rows(sbv_m, META_PAD),
                   bsz, seq, tq)
    ymla = _mla_call(qm, km, vm, _pad_rows(km_m, META_PAD), _pad_rows(vm_m, META_PAD),
                     bsz, seq, tq)

    w4 = [g(p["mix_pre_g"]), _bf(p["w_in"][:, 2208:]), g(p["b_gate"]), _bf(p["w_sb_o"]),
          _bf(p["w_mla_o"]), _bf(p["w_out"]), g(p["mix_post_g"]), g(p["ffn2_pre_g"]),
          _bf(p["ffn2_w_gate"]), _bf(p["ffn2_w_up"]), _bf(p["ffn2_w_down"]),
          g(p["ffn2_post_g"])]
    out = _merge_ffn_call(h1, ysb, ymla, w4, tm)
    return out.reshape(bsz, seq, d)


def kernel(x, meta_tokens, ffn1_pre_g, ffn1_w_gate, ffn1_w_up, ffn1_w_down, ffn1_post_g, mix_pre_g, w_in, b_gate, q_norm_g, w_uq, kv_norm_g, w_ukv, w_sb_o, w_mla_o, w_out, mix_post_g, ffn2_pre_g, ffn2_w_gate, ffn2_w_up, ffn2_w_down, ffn2_post_g):
    names = ("ffn1_pre_g", "ffn1_w_gate", "ffn1_w_up", "ffn1_w_down", "ffn1_post_g", "mix_pre_g",
             "w_in", "b_gate", "q_norm_g", "w_uq", "kv_norm_g", "w_ukv", "w_sb_o", "w_mla_o",
             "w_out", "mix_post_g", "ffn2_pre_g", "ffn2_w_gate", "ffn2_w_up", "ffn2_w_down",
             "ffn2_post_g")
    stacked = (ffn1_pre_g, ffn1_w_gate, ffn1_w_up, ffn1_w_down, ffn1_post_g, mix_pre_g, w_in,
               b_gate, q_norm_g, w_uq, kv_norm_g, w_ukv, w_sb_o, w_mla_o, w_out, mix_post_g,
               ffn2_pre_g, ffn2_w_gate, ffn2_w_up, ffn2_w_down, ffn2_post_g)
    assert all(w.shape[0] == 1 for w in stacked), "single-layer block"
    seq = x.shape[1]
    tq = 2 * SB_TK
    tm = 512
    assert seq % tq == 0 and (x.shape[0] * seq) % tm == 0 and seq % tm == 0
    p = {n: w[0] for n, w in zip(names, stacked)}
    return _layer(x, meta_tokens.astype(x.dtype), p, tm, tq)
```

```python
import functools

import jax
import jax.numpy as jnp
from jax import lax
from jax.experimental import pallas as pl
from jax.experimental.pallas import tpu as pltpu

D_MODEL = 1024
D_FF = 2816
N_META = 16
CHUNK = 64
SB_WIDTH = 512
MLA_HEADS = 8
MLA_Q_LORA = 384
MLA_KV_LORA = 256
MLA_NOPE = 64
MLA_ROPE = 32
MLA_V = 64
ROPE_THETA = 10000.0
EPS = 1e-6
NEG_INF = -1e30
LOG2_E = 1.4426950408889634
HEAD_PAIR = 128
MLA_QK_PAD = 128
META_PAD = 128
SB_TK = 256
VMEM_LIMIT = 56 * 1024 * 1024

F32 = jnp.float32
BF16 = jnp.bfloat16


def _bf(x):
    return x.astype(BF16)


def _dot(a, b):
    return jnp.dot(a, b, preferred_element_type=F32)


def _dot_nt(a, b):
    return lax.dot_general(a, b, (((1,), (1,)), ((), ())), preferred_element_type=F32)


def _rms(x, g):
    return x * lax.rsqrt(jnp.mean(x * x, axis=-1, keepdims=True) + EPS) * g


def _row_stat(x):
    return jnp.broadcast_to(x, (x.shape[0], HEAD_PAIR))


def _across_lanes(stat, width):
    return stat if width == HEAD_PAIR else jnp.concatenate([stat] * (width // HEAD_PAIR), axis=1)


def _ffn(h, pre_g, wg_ref, wu_ref, wd_ref, post_g):
    xn = _bf(_rms(h, pre_g))
    g = _dot(xn, wg_ref[...])
    u = _dot(xn, wu_ref[...])
    a = _bf(g * jax.nn.sigmoid(g) * u)
    f = _dot(a, wd_ref[...])
    return h + 0.5 * _rms(f, post_g)


def _ffn_proj_kernel(x_ref, cs_ref, sn_ref, pre_g, wg, wu, wd, post_g, mix_g, w_a, qn_g, w_q,
                     kvn_g, w_kv, h1_ref, sbq_ref, sbk_ref, sbv_ref, qm_ref, km_ref, vm_ref):
    h1 = _ffn(x_ref[...], pre_g[...], wg, wu, wd, post_g[...])
    h1_ref[...] = h1
    u = _bf(_rms(h1, mix_g[...]))
    p = _dot(u, w_a[...])
    sbq_ref[...] = _bf(p[:, 0:512])
    sbk_ref[...] = _bf(p[:, 512:1024])
    sbv_ref[...] = _bf(p[:, 1024:1536])
    c_q = p[:, 1536:1920]
    c_kv = p[:, 1920:2176]
    kr = p[:, 2176:2304]
    kr_rot = p[:, 2304:2432]
    cs = cs_ref[...]
    sn = sn_ref[...]
    scale = (MLA_NOPE + MLA_ROPE) ** -0.5 * LOG2_E

    q2 = _dot(_bf(_rms(c_q, qn_g[...])), w_q[...])
    half = MLA_HEADS * MLA_QK_PAD
    for h in range(MLA_HEADS):
        lo, hi = h * MLA_QK_PAD, (h + 1) * MLA_QK_PAD
        qm_ref[:, lo:hi] = _bf((q2[:, lo:hi] * cs + q2[:, half + lo:half + hi] * sn) * scale)

    kv = _dot(_bf(_rms(c_kv, kvn_g[...])), w_kv[...])
    k_rope = kr * cs + kr_rot * sn
    for h in range(MLA_HEADS):
        lo, hi = h * MLA_QK_PAD, (h + 1) * MLA_QK_PAD
        km_ref[:, lo:hi] = _bf(kv[:, lo:hi] + k_rope)
    vm_ref[...] = _bf(kv[:, half:])


def _const_spec(shape):
    return pl.BlockSpec(shape, lambda *_: (0,) * len(shape), pipeline_mode=pl.Buffered(1))


def _ffn_proj_call(x2d, cs, sn, weights, tm):
    t = x2d.shape[0]
    n_pos_blocks = cs.shape[0] // tm
    row = lambda w: pl.BlockSpec((tm, w), lambda i: (i, 0))
    pos = pl.BlockSpec((tm, MLA_QK_PAD), lambda i: (i % n_pos_blocks, 0))
    in_specs = [row(D_MODEL), pos, pos] + [_const_spec(w.shape) for w in weights]
    widths = (D_MODEL, SB_WIDTH, SB_WIDTH, SB_WIDTH, MLA_HEADS * MLA_QK_PAD,
              MLA_HEADS * MLA_QK_PAD, MLA_HEADS * MLA_V)
    dtypes = (F32,) + (BF16,) * 6
    return pl.pallas_call(
        _ffn_proj_kernel,
        grid=(t // tm,),
        in_specs=in_specs,
        out_specs=[row(w) for w in widths],
        out_shape=[jax.ShapeDtypeStruct((t, w), d) for w, d in zip(widths, dtypes)],
        compiler_params=pltpu.CompilerParams(dimension_semantics=("parallel",),
                                             vmem_limit_bytes=VMEM_LIMIT),
        name="ffn1_proj",
    )(x2d, cs, sn, *weights)


def _sb_tile(qh, kblk, vblk, ntri, mask):
    z2 = _dot_nt(qh, kblk)
    sp = jnp.maximum(z2, 0.0) + jnp.log2(1.0 + jnp.exp2(-jnp.abs(z2)))
    log2_b = z2 - sp
    if mask is not None:
        sp = jnp.where(mask, sp, 0.0)
    a = jnp.exp2(log2_b + _dot(_bf(sp), ntri))
    if mask is not None:
        a = jnp.where(mask, a, 0.0)
    return _dot(_bf(a), vblk), _row_stat(jnp.sum(sp, axis=-1, keepdims=True))


def _sb_kernel(q_ref, k_ref, v_ref, mk_ref, mv_ref, ntri_ref, o_ref, *, tq, nq):
    for n in range(nq):
        pl.when(pl.program_id(2) == n)(
            functools.partial(_sb_query_tile, n, q_ref, k_ref, v_ref, mk_ref, mv_ref, ntri_ref,
                              o_ref, tq))


def _sb_query_tile(n_blocks, q_ref, k_ref, v_ref, mk_ref, mv_ref, ntri_ref, o_ref, tq):
    tk = SB_TK
    assert tq == 2 * tk
    start = n_blocks * tq
    lane = lax.broadcasted_iota(jnp.int32, (tq, HEAD_PAIR), 1)
    row = lax.broadcasted_iota(jnp.int32, (tq, tk), 0)
    col = lax.broadcasted_iota(jnp.int32, (tq, tk), 1)
    far_mask = (row >= tk) | (col < row)
    near_mask = lax.broadcasted_iota(jnp.int32, (tk, tk), 1) < lax.broadcasted_iota(
        jnp.int32, (tk, tk), 0)
    meta_mask = lax.broadcasted_iota(jnp.int32, (tq, META_PAD), 1) < N_META
    q = q_ref[...]
    ntri = ntri_ref[...]
    zero = jnp.zeros_like(q)
    qs = [jnp.where(lane < 64, q, zero), jnp.where(lane >= 64, q, zero)]

    def kv(s):
        return k_ref[pl.ds(s, tk), :], v_ref[pl.ds(s, tk), :]

    def add(state, pv, sp_sum):
        acc, log2_gap = state
        return acc + jnp.exp2(log2_gap) * pv, log2_gap - sp_sum

    accs = []
    for qh in qs:
        pv, sp_sum = _sb_tile(qh[tk:], *kv(start + tk), ntri, near_mask)
        top = jnp.zeros((tk, HEAD_PAIR), F32)
        state = (jnp.concatenate([top, pv], axis=0), jnp.concatenate([top, -sp_sum], axis=0))
        state = add(state, *_sb_tile(qh, *kv(start), ntri, far_mask))
        for s in range(start - tk, -1, -tk):
            state = add(state, *_sb_tile(qh, *kv(s), ntri, None))
        state = add(state, *_sb_tile(qh, mk_ref[...], mv_ref[...], ntri[:META_PAD, :META_PAD],
                                     meta_mask))
        accs.append(state[0])
    o_ref[...] = _bf(jnp.where(lane < 64, accs[0], accs[1]))


def _sb_call(sbq, sbk, sbv, mk, mv, bsz, seq, tq):
    nq = seq // tq
    ntri = -(lax.broadcasted_iota(jnp.int32, (SB_TK, SB_TK), 0)
             > lax.broadcasted_iota(jnp.int32, (SB_TK, SB_TK), 1)).astype(BF16)
    qspec = pl.BlockSpec((tq, HEAD_PAIR), lambda b, h, i: (b * nq + i, h))
    kvspec = pl.BlockSpec((seq, HEAD_PAIR), lambda b, h, i: (b, h))
    mspec = pl.BlockSpec((META_PAD, HEAD_PAIR), lambda b, h, i: (0, h))
    return pl.pallas_call(
        functools.partial(_sb_kernel, tq=tq, nq=nq),
        grid=(bsz, SB_WIDTH // HEAD_PAIR, nq),
        in_specs=[qspec, kvspec, kvspec, mspec, mspec,
                  pl.BlockSpec((SB_TK, SB_TK), lambda b, h, i: (0, 0))],
        out_specs=qspec,
        out_shape=jax.ShapeDtypeStruct(sbq.shape, BF16),
        compiler_params=pltpu.CompilerParams(
            dimension_semantics=("parallel", "parallel", "arbitrary"),
            vmem_limit_bytes=VMEM_LIMIT),
        name="sb_attention",
    )(sbq, sbk, sbv, mk, mv, ntri)


def _softmax_step(q, blocks, state):
    scores = []
    for kblk, _, mask in blocks:
        s = _dot_nt(q, kblk)
        scores.append(s if mask is None else jnp.where(mask, s, NEG_INF))
    m_new = functools.reduce(jnp.maximum, [_row_stat(jnp.max(s, axis=-1, keepdims=True))
                                           for s in scores])
    if state is not None:
        m, l, acc = state
        m_new = jnp.maximum(m, m_new)
        alpha = jnp.exp2(m - m_new)
    probs = [jnp.exp2(s - _across_lanes(m_new, s.shape[1])) for s in scores]
    l_terms = [_row_stat(jnp.sum(p, axis=-1, keepdims=True)) for p in probs]
    pv_terms = [_dot(_bf(p), vblk) for p, (_, vblk, _) in zip(probs, blocks)]
    if state is not None:
        l_terms.insert(0, alpha * l)
        pv_terms.insert(0, alpha * acc)
    return (m_new, functools.reduce(lambda a, b: a + b, l_terms),
            functools.reduce(lambda a, b: a + b, pv_terms))


def _mla_kernel(q_ref, k_ref, v_ref, mk_ref, mv_ref, o_ref, *, tq, nq):
    for n in range(nq):
        pl.when(pl.program_id(2) == n)(
            functools.partial(_mla_tile, n, q_ref, k_ref, v_ref, mk_ref, mv_ref, o_ref, tq))


def _mla_tile(n_blocks, q_ref, k_ref, v_ref, mk_ref, mv_ref, o_ref, tq):
    half = tq // 2
    start = n_blocks * tq
    lane = lax.broadcasted_iota(jnp.int32, (tq, HEAD_PAIR), 1)
    row = lax.broadcasted_iota(jnp.int32, (half, tq), 0)
    col = lax.broadcasted_iota(jnp.int32, (half, tq), 1)
    top_mask = ((col // CHUNK) <= (row // CHUNK))[:, :half]
    bot_mask = (col // CHUNK) <= ((row + half) // CHUNK)
    meta_mask = lax.broadcasted_iota(jnp.int32, (half, META_PAD), 1) < N_META
    lanes = [(hh * MLA_QK_PAD, (hh + 1) * MLA_QK_PAD) for hh in range(2)]
    qs = [q_ref[:, lo:hi] for lo, hi in lanes]

    states = []
    for q, (lo, hi) in zip(qs, lanes):
        meta = (mk_ref[:, lo:hi], mv_ref[...], meta_mask)
        top = _softmax_step(q[:half], [meta, (k_ref[pl.ds(start, half), lo:hi],
                                              v_ref[pl.ds(start, half), :], top_mask)], None)
        bot = _softmax_step(q[half:], [meta, (k_ref[pl.ds(start, tq), lo:hi],
                                              v_ref[pl.ds(start, tq), :], bot_mask)], None)
        states.append(tuple(jnp.concatenate([t, b], axis=0) for t, b in zip(top, bot)))

    for j in range(n_blocks):
        s = j * tq
        states = [_softmax_step(q, [(k_ref[pl.ds(s, tq), lo:hi], v_ref[pl.ds(s, tq), :], None)],
                                state)
                  for q, (lo, hi), state in zip(qs, lanes, states)]
    outs = [acc / l for _, l, acc in states]
    o_ref[...] = _bf(jnp.where(lane < 64, outs[0], outs[1]))


def _mla_call(qm, km, vm, mk, mv, bsz, seq, tq):
    nq = seq // tq
    pair = 2 * MLA_QK_PAD
    return pl.pallas_call(
        functools.partial(_mla_kernel, tq=tq, nq=nq),
        grid=(bsz, MLA_HEADS // 2, nq),
        in_specs=[pl.BlockSpec((tq, pair), lambda b, h, i: (b * nq + i, h)),
                  pl.BlockSpec((seq, pair), lambda b, h, i: (b, h)),
                  pl.BlockSpec((seq, HEAD_PAIR), lambda b, h, i: (b, h)),
                  pl.BlockSpec((META_PAD, pair), lambda b, h, i: (0, h)),
                  pl.BlockSpec((META_PAD, HEAD_PAIR), lambda b, h, i: (0, h))],
        out_specs=pl.BlockSpec((tq, HEAD_PAIR), lambda b, h, i: (b * nq + i, h)),
        out_shape=jax.ShapeDtypeStruct(vm.shape, BF16),
        compiler_params=pltpu.CompilerParams(
            dimension_semantics=("parallel", "parallel", "arbitrary"),
            vmem_limit_bytes=VMEM_LIMIT),
        name="mla_attention",
    )(qm, km, vm, mk, mv)


def _merge_ffn_kernel(h1_ref, ysb_ref, ymla_ref, mix_g, w_gate, b_gate, w_sbo, w_mlao, w_out,
                      mixp_g, pre_g, wg, wu, wd, post_g, o_ref):
    h1 = h1_ref[...]
    u = _bf(_rms(h1, mix_g[...]))
    gate = jax.nn.sigmoid(_dot(u, w_gate[...]) + b_gate[...])
    y_sb = _dot(ysb_ref[...], w_sbo[...])
    y_mla = _dot(ymla_ref[...], w_mlao[...])
    merged = gate[:, :D_MODEL] * y_sb + gate[:, D_MODEL:] * y_mla
    m = _dot(_bf(merged), w_out[...])
    h2 = h1 + _rms(m, mixp_g[...])
    o_ref[...] = _ffn(h2, pre_g[...], wg, wu, wd, post_g[...])


def _merge_ffn_call(h1, ysb, ymla, weights, tm):
    t = h1.shape[0]
    row = lambda w: pl.BlockSpec((tm, w), lambda i: (i, 0))
    return pl.pallas_call(
        _merge_ffn_kernel,
        grid=(t // tm,),
        in_specs=[row(D_MODEL), row(SB_WIDTH), row(SB_WIDTH)]
                 + [_const_spec(w.shape) for w in weights],
        out_specs=row(D_MODEL),
        out_shape=jax.ShapeDtypeStruct((t, D_MODEL), F32),
        compiler_params=pltpu.CompilerParams(dimension_semantics=("parallel",),
                                             vmem_limit_bytes=VMEM_LIMIT),
        name="merge_ffn2",
    )(h1, ysb, ymla, *weights)


def _rope_tables(pos):
    half = MLA_ROPE // 2
    inv = ROPE_THETA ** (-jnp.arange(half, dtype=F32) / half)
    ang = pos.astype(F32)[:, None] * inv[None, :]
    cos, sin = jnp.cos(ang), jnp.sin(ang)
    n = pos.shape[0]
    pad = jnp.zeros((n, MLA_QK_PAD - MLA_NOPE - MLA_ROPE), F32)
    cs = jnp.concatenate([jnp.ones((n, MLA_NOPE), F32), cos, cos, pad], axis=1)
    sn = jnp.concatenate([jnp.zeros((n, MLA_NOPE), F32), sin, sin, pad], axis=1)
    return cs, sn


def _rotate_half_cols(w):
    half = MLA_ROPE // 2
    return jnp.concatenate([-w[..., half:], w[..., :half]], axis=-1)


def _mixer_in_weights(w_in, w_uq, w_ukv):
    d = w_in.shape[0]
    w_sbq = w_in[:, 0:512] * (64 ** -0.5 * LOG2_E)
    w_kr = w_in[:, 2176:2208]
    z_nope = jnp.zeros((d, MLA_NOPE), F32)
    z_pad = jnp.zeros((d, MLA_QK_PAD - MLA_NOPE - MLA_ROPE), F32)
    w_a = jnp.concatenate([w_sbq, w_in[:, 512:2176],
                           z_nope, w_kr, z_pad,
                           z_nope, _rotate_half_cols(w_kr), z_pad], axis=1)
    q_nope, q_rope = w_uq[:, :, :MLA_NOPE], w_uq[:, :, MLA_NOPE:]
    zq_nope = jnp.zeros_like(q_nope)
    zq_pad = jnp.zeros(q_rope.shape[:2] + (MLA_QK_PAD - MLA_NOPE - MLA_ROPE,), F32)
    w_q = jnp.concatenate([
        jnp.concatenate([q_nope, q_rope, zq_pad], -1).reshape(MLA_Q_LORA, -1),
        jnp.concatenate([zq_nope, _rotate_half_cols(q_rope), zq_pad], -1).reshape(MLA_Q_LORA, -1),
    ], axis=1)
    k_nope, v = w_ukv[:, :, :MLA_NOPE], w_ukv[:, :, MLA_NOPE:]
    w_kv = jnp.concatenate([
        jnp.concatenate([k_nope, jnp.zeros_like(k_nope)], -1).reshape(MLA_KV_LORA, -1),
        v.reshape(MLA_KV_LORA, -1)], axis=1)
    return _bf(w_a), _bf(w_q), _bf(w_kv)


def _pad_rows(a, rows):
    return jnp.pad(a, ((0, rows - a.shape[0]), (0, 0)))


def _layer(h, meta, p, tm, tq):
    bsz, seq, d = h.shape
    g = lambda v: v.reshape(1, -1)
    w_a, w_q, w_kv = _mixer_in_weights(p["w_in"], p["w_uq"], p["w_ukv"])
    w1 = [g(p["ffn1_pre_g"]), _bf(p["ffn1_w_gate"]), _bf(p["ffn1_w_up"]), _bf(p["ffn1_w_down"]),
          g(p["ffn1_post_g"]), g(p["mix_pre_g"]), w_a, g(p["q_norm_g"]), w_q,
          g(p["kv_norm_g"]), w_kv]
    cs, sn = _rope_tables(N_META + jnp.arange(seq))
    cs_m, sn_m = _rope_tables(jnp.arange(N_META))

    h1, sbq, sbk, sbv, qm, km, vm = _ffn_proj_call(h.reshape(bsz * seq, d), cs, sn, w1, tm)
    _, _, sbk_m, sbv_m, _, km_m, vm_m = _ffn_proj_call(meta, cs_m, sn_m, w1, N_META)

    ysb = _sb_call(sbq, sbk, sbv, _pad_rows(sbk_m, META_PAD), _pad_rows(sbv_m, META_PAD),
                   bsz, seq, tq)
    ymla = _mla_call(qm, km, vm, _pad_rows(km_m, META_PAD), _pad_rows(vm_m, META_PAD),
                     bsz, seq, tq)

    w4 = [g(p["mix_pre_g"]), _bf(p["w_in"][:, 2208:]), g(p["b_gate"]), _bf(p["w_sb_o"]),
          _bf(p["w_mla_o"]), _bf(p["w_out"]), g(p["mix_post_g"]), g(p["ffn2_pre_g"]),
          _bf(p["ffn2_w_gate"]), _bf(p["ffn2_w_up"]), _bf(p["ffn2_w_down"]),
          g(p["ffn2_post_g"])]
    out = _merge_ffn_call(h1, ysb, ymla, w4, tm)
    return out.reshape(bsz, seq, d)


def kernel(x, meta_tokens, ffn1_pre_g, ffn1_w_gate, ffn1_w_up, ffn1_w_down, ffn1_post_g, mix_pre_g, w_in, b_gate, q_norm_g, w_uq, kv_norm_g, w_ukv, w_sb_o, w_mla_o, w_out, mix_post_g, ffn2_pre_g, ffn2_w_gate, ffn2_w_up, ffn2_w_down, ffn2_post_g):
    names = ("ffn1_pre_g", "ffn1_w_gate", "ffn1_w_up", "ffn1_w_down", "ffn1_post_g", "mix_pre_g",
             "w_in", "b_gate", "q_norm_g", "w_uq", "kv_norm_g", "w_ukv", "w_sb_o", "w_mla_o",
             "w_out", "mix_post_g", "ffn2_pre_g", "ffn2_w_gate", "ffn2_w_up", "ffn2_w_down",
             "ffn2_post_g")
    stacked = (ffn1_pre_g, ffn1_w_gate, ffn1_w_up, ffn1_w_down, ffn1_post_g, mix_pre_g, w_in,
               b_gate, q_norm_g, w_uq, kv_norm_g, w_ukv, w_sb_o, w_mla_o, w_out, mix_post_g,
               ffn2_pre_g, ffn2_w_gate, ffn2_w_up, ffn2_w_down, ffn2_post_g)
    assert all(w.shape[0] == 1 for w in stacked), "single-layer block"
    seq = x.shape[1]
    tq = 2 * SB_TK
    tm = 512
    assert seq % tq == 0 and (x.shape[0] * seq) % tm == 0 and seq % tm == 0
    p = {n: w[0] for n, w in zip(names, stacked)}
    return _layer(x, meta_tokens.astype(x.dtype), p, tm, tq)
```

```python
import functools

import jax
import jax.numpy as jnp
from jax import lax
from jax.experimental import pallas as pl
from jax.experimental.pallas import tpu as pltpu

D_MODEL = 1024
D_FF = 2816
N_META = 16
CHUNK = 64
SB_WIDTH = 512
MLA_HEADS = 8
MLA_Q_LORA = 384
MLA_KV_LORA = 256
MLA_NOPE = 64
MLA_ROPE = 32
MLA_V = 64
ROPE_THETA = 10000.0
EPS = 1e-6
NEG_INF = -1e30
LOG2_E = 1.4426950408889634
HEAD_PAIR = 128
MLA_QK_PAD = 128
META_PAD = 128
SB_TK = 256
ROW_GROUPS = 2
MIN_GROUP_ROWS = 128
VMEM_LIMIT = 56 * 1024 * 1024

F32 = jnp.float32
BF16 = jnp.bfloat16


def _bf(x):
    return x.astype(BF16)


def _dot(a, b):
    return jnp.dot(a, b, preferred_element_type=F32)


def _dot_nt(a, b):
    return lax.dot_general(a, b, (((1,), (1,)), ((), ())), preferred_element_type=F32)


def _rms(x, g):
    return x * lax.rsqrt(jnp.mean(x * x, axis=-1, keepdims=True) + EPS) * g


def _row_stat(x):
    return jnp.broadcast_to(x, (x.shape[0], HEAD_PAIR))


def _across_lanes(stat, width):
    return stat if width == HEAD_PAIR else jnp.concatenate([stat] * (width // HEAD_PAIR), axis=1)


def _trace_pipelined(stage_gens, on_done):
    done = [False] * len(stage_gens)
    step = 0
    while not all(done):
        for g, gen in enumerate(stage_gens):
            if step >= g and not done[g]:
                try:
                    next(gen)
                except StopIteration as stop:
                    done[g] = True
                    on_done(g, stop.value)
        step += 1


def _ffn(h, pre_g, wg_ref, wu_ref, wd_ref, post_g):
    xn = _bf(_rms(h, pre_g))
    yield
    g = _dot(xn, wg_ref[...])
    u = _dot(xn, wu_ref[...])
    yield
    f = _dot(_bf(g * jax.nn.sigmoid(g) * u), wd_ref[...])
    yield
    return h + 0.5 * _rms(f, post_g)


def _trace_row_groups(n_rows, chain):
    rows = n_rows // ROW_GROUPS if n_rows % (ROW_GROUPS * MIN_GROUP_ROWS) == 0 else n_rows
    _trace_pipelined([chain(r, rows) for r in range(0, n_rows, rows)], lambda g, value: None)


def _ffn_proj_kernel(x_ref, cs_ref, sn_ref, pre_g, wg, wu, wd, post_g, mix_g, w_a, qn_g, w_q,
                     kvn_g, w_kv, h1_ref, sbq_ref, sbk_ref, sbv_ref, qm_ref, km_ref, vm_ref):
    scale = (MLA_NOPE + MLA_ROPE) ** -0.5 * LOG2_E
    half = MLA_HEADS * MLA_QK_PAD

    def chain(r, rows):
        rs = slice(r, r + rows)
        h1 = yield from _ffn(x_ref[rs, :], pre_g[...], wg, wu, wd, post_g[...])
        h1_ref[rs, :] = h1
        u = _bf(_rms(h1, mix_g[...]))
        yield
        p = _dot(u, w_a[...])
        sbq_ref[rs, :] = _bf(p[:, 0:512])
        sbk_ref[rs, :] = _bf(p[:, 512:1024])
        sbv_ref[rs, :] = _bf(p[:, 1024:1536])
        c_q = _bf(_rms(p[:, 1536:1920], qn_g[...]))
        c_kv = _bf(_rms(p[:, 1920:2176], kvn_g[...]))
        kr = p[:, 2176:2304]
        kr_rot = p[:, 2304:2432]
        yield
        q2 = _dot(c_q, w_q[...])
        kv = _dot(c_kv, w_kv[...])
        yield
        cs = cs_ref[rs, :]
        sn = sn_ref[rs, :]
        k_rope = kr * cs + kr_rot * sn
        for h in range(MLA_HEADS):
            lo, hi = h * MLA_QK_PAD, (h + 1) * MLA_QK_PAD
            qm_ref[rs, lo:hi] = _bf((q2[:, lo:hi] * cs + q2[:, half + lo:half + hi] * sn) * scale)
            km_ref[rs, lo:hi] = _bf(kv[:, lo:hi] + k_rope)
        vm_ref[rs, :] = _bf(kv[:, half:])

    _trace_row_groups(x_ref.shape[0], chain)


def _const_spec(shape):
    return pl.BlockSpec(shape, lambda *_: (0,) * len(shape), pipeline_mode=pl.Buffered(1))


def _ffn_proj_call(x2d, cs, sn, weights, tm):
    t = x2d.shape[0]
    n_pos_blocks = cs.shape[0] // tm
    row = lambda w: pl.BlockSpec((tm, w), lambda i: (i, 0))
    pos = pl.BlockSpec((tm, MLA_QK_PAD), lambda i: (i % n_pos_blocks, 0))
    in_specs = [row(D_MODEL), pos, pos] + [_const_spec(w.shape) for w in weights]
    widths = (D_MODEL, SB_WIDTH, SB_WIDTH, SB_WIDTH, MLA_HEADS * MLA_QK_PAD,
              MLA_HEADS * MLA_QK_PAD, MLA_HEADS * MLA_V)
    dtypes = (F32,) + (BF16,) * 6
    return pl.pallas_call(
        _ffn_proj_kernel,
        grid=(t // tm,),
        in_specs=in_specs,
        out_specs=[row(w) for w in widths],
        out_shape=[jax.ShapeDtypeStruct((t, w), d) for w, d in zip(widths, dtypes)],
        compiler_params=pltpu.CompilerParams(dimension_semantics=("parallel",),
                                             vmem_limit_bytes=VMEM_LIMIT),
        name="ffn1_proj",
    )(x2d, cs, sn, *weights)


def _sb_tile(qh, load_kv, ntri, mask):
    kblk, vblk = load_kv()
    z2 = _dot_nt(qh, kblk)
    sp = jnp.maximum(z2, 0.0) + jnp.log2(1.0 + jnp.exp2(-jnp.abs(z2)))
    log2_b = z2 - sp
    if mask is not None:
        sp = jnp.where(mask, sp, 0.0)
    sp_sum = _row_stat(jnp.sum(sp, axis=-1, keepdims=True))
    sp = _bf(sp)
    yield
    a = jnp.exp2(log2_b + _dot(sp, ntri))
    if mask is not None:
        a = jnp.where(mask, a, 0.0)
    a = _bf(a)
    yield
    return _dot(a, vblk), sp_sum


def _sb_kernel(q_ref, k_ref, v_ref, mk_ref, mv_ref, ntri_ref, o_ref, *, tq, nq):
    for n in range(nq):
        pl.when(pl.program_id(2) == n)(
            functools.partial(_sb_query_tile, n, q_ref, k_ref, v_ref, mk_ref, mv_ref, ntri_ref,
                              o_ref, tq))


def _sb_query_tile(n_blocks, q_ref, k_ref, v_ref, mk_ref, mv_ref, ntri_ref, o_ref, tq):
    tk = SB_TK
    assert tq == 2 * tk
    start = n_blocks * tq
    lane = lax.broadcasted_iota(jnp.int32, (tq, HEAD_PAIR), 1)
    row = lax.broadcasted_iota(jnp.int32, (tq, tk), 0)
    col = lax.broadcasted_iota(jnp.int32, (tq, tk), 1)
    far_mask = (row >= tk) | (col < row)
    near_mask = lax.broadcasted_iota(jnp.int32, (tk, tk), 1) < lax.broadcasted_iota(
        jnp.int32, (tk, tk), 0)
    meta_mask = lax.broadcasted_iota(jnp.int32, (tq, META_PAD), 1) < N_META
    q = q_ref[...]
    ntri = ntri_ref[...]
    zero = jnp.zeros_like(q)
    qs = [jnp.where(lane < 64, q, zero), jnp.where(lane >= 64, q, zero)]

    def kv(s):
        return lambda: (k_ref[pl.ds(s, tk), :], v_ref[pl.ds(s, tk), :])

    def add(state, pv, sp_sum):
        acc, log2_gap = state
        return acc + jnp.exp2(log2_gap) * pv, log2_gap - sp_sum

    def tiles(qh):
        yield _sb_tile(qh[tk:], kv(start + tk), ntri, near_mask)
        yield _sb_tile(qh, kv(start), ntri, far_mask)
        for s in range(start - tk, -1, -tk):
            yield _sb_tile(qh, kv(s), ntri, None)
        yield _sb_tile(qh, lambda: (mk_ref[...], mv_ref[...]), ntri[:META_PAD, :META_PAD],
                       meta_mask)

    states = [None, None]

    def done(g, result):
        hh = g % 2
        if states[hh] is None:
            top = jnp.zeros((tk, HEAD_PAIR), F32)
            states[hh] = (jnp.concatenate([top, result[0]], axis=0),
                          jnp.concatenate([top, -result[1]], axis=0))
        else:
            states[hh] = add(states[hh], *result)

    _trace_pipelined([t for pair in zip(tiles(qs[0]), tiles(qs[1])) for t in pair], done)
    o_ref[...] = _bf(jnp.where(lane < 64, states[0][0], states[1][0]))


def _sb_call(sbq, sbk, sbv, mk, mv, bsz, seq, tq):
    nq = seq // tq
    ntri = -(lax.broadcasted_iota(jnp.int32, (SB_TK, SB_TK), 0)
             > lax.broadcasted_iota(jnp.int32, (SB_TK, SB_TK), 1)).astype(BF16)
    qspec = pl.BlockSpec((tq, HEAD_PAIR), lambda b, h, i: (b * nq + i, h))
    kvspec = pl.BlockSpec((seq, HEAD_PAIR), lambda b, h, i: (b, h))
    mspec = pl.BlockSpec((META_PAD, HEAD_PAIR), lambda b, h, i: (0, h))
    return pl.pallas_call(
        functools.partial(_sb_kernel, tq=tq, nq=nq),
        grid=(bsz, SB_WIDTH // HEAD_PAIR, nq),
        in_specs=[qspec, kvspec, kvspec, mspec, mspec,
                  pl.BlockSpec((SB_TK, SB_TK), lambda b, h, i: (0, 0))],
        out_specs=qspec,
        out_shape=jax.ShapeDtypeStruct(sbq.shape, BF16),
        compiler_params=pltpu.CompilerParams(
            dimension_semantics=("parallel", "parallel", "arbitrary"),
            vmem_limit_bytes=VMEM_LIMIT),
        name="sb_attention",
    )(sbq, sbk, sbv, mk, mv, ntri)


def _softmax_step(q, load_blocks, read_state):
    blocks = load_blocks()
    scores = []
    for kblk, _, mask in blocks:
        s = _dot_nt(q, kblk)
        scores.append(s if mask is None else jnp.where(mask, s, NEG_INF))
    m_new = functools.reduce(jnp.maximum, [_row_stat(jnp.max(s, axis=-1, keepdims=True))
                                           for s in scores])
    yield
    state = read_state()
    if state is not None:
        m, l, acc = state
        m_new = jnp.maximum(m, m_new)
        alpha = jnp.exp2(m - m_new)
    probs = [jnp.exp2(s - _across_lanes(m_new, s.shape[1])) for s in scores]
    l_terms = [_row_stat(jnp.sum(p, axis=-1, keepdims=True)) for p in probs]
    probs = [_bf(p) for p in probs]
    yield
    pv_terms = [_dot(p, vblk) for p, (_, vblk, _) in zip(probs, blocks)]
    if state is not None:
        l_terms.insert(0, alpha * l)
        pv_terms.insert(0, alpha * acc)
    return (m_new, functools.reduce(lambda a, b: a + b, l_terms),
            functools.reduce(lambda a, b: a + b, pv_terms))


def _mla_kernel(q_ref, k_ref, v_ref, mk_ref, mv_ref, o_ref, *, tq, nq):
    for n in range(nq):
        pl.when(pl.program_id(2) == n)(
            functools.partial(_mla_tile, n, q_ref, k_ref, v_ref, mk_ref, mv_ref, o_ref, tq))


def _mla_tile(n_blocks, q_ref, k_ref, v_ref, mk_ref, mv_ref, o_ref, tq):
    half = tq // 2
    start = n_blocks * tq
    lane = lax.broadcasted_iota(jnp.int32, (tq, HEAD_PAIR), 1)
    row = lax.broadcasted_iota(jnp.int32, (half, tq), 0)
    col = lax.broadcasted_iota(jnp.int32, (half, tq), 1)
    top_mask = ((col // CHUNK) <= (row // CHUNK))[:, :half]
    bot_mask = (col // CHUNK) <= ((row + half) // CHUNK)
    meta_mask = lax.broadcasted_iota(jnp.int32, (half, META_PAD), 1) < N_META
    lanes = [(hh * MLA_QK_PAD, (hh + 1) * MLA_QK_PAD) for hh in range(2)]
    qs = [q_ref[:, lo:hi] for lo, hi in lanes]

    states = [{}, {}]

    def diag(hh, part, rows, keys, mask):
        lo, hi = lanes[hh]
        load = lambda: [(mk_ref[:, lo:hi], mv_ref[...], meta_mask),
                        (k_ref[pl.ds(start, keys), lo:hi], v_ref[pl.ds(start, keys), :], mask)]
        return part, _softmax_step(qs[hh][rows], load, lambda: None)

    def joined(hh):
        return tuple(jnp.concatenate([t, b], axis=0)
                     for t, b in zip(states[hh]["top"], states[hh]["bot"]))

    def block(hh, j):
        lo, hi = lanes[hh]
        load = lambda: [(k_ref[pl.ds(j * tq, tq), lo:hi], v_ref[pl.ds(j * tq, tq), :], None)]
        read = (lambda: joined(hh)) if j == 0 else (lambda: states[hh]["all"])
        return "all", _softmax_step(qs[hh], load, read)

    per_head = [[diag(hh, "top", slice(0, half), half, top_mask),
                 diag(hh, "bot", slice(half, tq), tq, bot_mask)]
                + [block(hh, j) for j in range(n_blocks)] for hh in range(2)]
    tiles = [t for pair in zip(*per_head) for t in pair]

    def done(g, state):
        states[g % 2][tiles[g][0]] = state

    _trace_pipelined([gen for _, gen in tiles], done)
    outs = []
    for hh in range(2):
        _, l, acc = states[hh]["all"] if n_blocks else joined(hh)
        outs.append(acc / l)
    o_ref[...] = _bf(jnp.where(lane < 64, outs[0], outs[1]))


def _mla_call(qm, km, vm, mk, mv, bsz, seq, tq):
    nq = seq // tq
    pair = 2 * MLA_QK_PAD
    return pl.pallas_call(
        functools.partial(_mla_kernel, tq=tq, nq=nq),
        grid=(bsz, MLA_HEADS // 2, nq),
        in_specs=[pl.BlockSpec((tq, pair), lambda b, h, i: (b * nq + i, h)),
                  pl.BlockSpec((seq, pair), lambda b, h, i: (b, h)),
                  pl.BlockSpec((seq, HEAD_PAIR), lambda b, h, i: (b, h)),
                  pl.BlockSpec((META_PAD, pair), lambda b, h, i: (0, h)),
                  pl.BlockSpec((META_PAD, HEAD_PAIR), lambda b, h, i: (0, h))],
        out_specs=pl.BlockSpec((tq, HEAD_PAIR), lambda b, h, i: (b * nq + i, h)),
        out_shape=jax.ShapeDtypeStruct(vm.shape, BF16),
        compiler_params=pltpu.CompilerParams(
            dimension_semantics=("parallel", "parallel", "arbitrary"),
            vmem_limit_bytes=VMEM_LIMIT),
        name="mla_attention",
    )(qm, km, vm, mk, mv)


def _merge_ffn_kernel(h1_ref, ysb_ref, ymla_ref, mix_g, w_gate, b_gate, w_sbo, w_mlao, w_out,
                      mixp_g, pre_g, wg, wu, wd, post_g, o_ref):
    def chain(r, rows):
        rs = slice(r, r + rows)
        h1 = h1_ref[rs, :]
        u = _bf(_rms(h1, mix_g[...]))
        yield
        gate = jax.nn.sigmoid(_dot(u, w_gate[...]) + b_gate[...])
        y_sb = _dot(ysb_ref[rs, :], w_sbo[...])
        y_mla = _dot(ymla_ref[rs, :], w_mlao[...])
        yield
        merged = gate[:, :D_MODEL] * y_sb + gate[:, D_MODEL:] * y_mla
        m = _dot(_bf(merged), w_out[...])
        yield
        h2 = h1 + _rms(m, mixp_g[...])
        o_ref[rs, :] = yield from _ffn(h2, pre_g[...], wg, wu, wd, post_g[...])

    _trace_row_groups(h1_ref.shape[0], chain)


def _merge_ffn_call(h1, ysb, ymla, weights, tm):
    t = h1.shape[0]
    row = lambda w: pl.BlockSpec((tm, w), lambda i: (i, 0))
    return pl.pallas_call(
        _merge_ffn_kernel,
        grid=(t // tm,),
        in_specs=[row(D_MODEL), row(SB_WIDTH), row(SB_WIDTH)]
                 + [_const_spec(w.shape) for w in weights],
        out_specs=row(D_MODEL),
        out_shape=jax.ShapeDtypeStruct((t, D_MODEL), F32),
        compiler_params=pltpu.CompilerParams(dimension_semantics=("parallel",),
                                             vmem_limit_bytes=VMEM_LIMIT),
        name="merge_ffn2",
    )(h1, ysb, ymla, *weights)


def _rope_tables(pos):
    half = MLA_ROPE // 2
    inv = ROPE_THETA ** (-jnp.arange(half, dtype=F32) / half)
    ang = pos.astype(F32)[:, None] * inv[None, :]
    cos, sin = jnp.cos(ang), jnp.sin(ang)
    n = pos.shape[0]
    pad = jnp.zeros((n, MLA_QK_PAD - MLA_NOPE - MLA_ROPE), F32)
    cs = jnp.concatenate([jnp.ones((n, MLA_NOPE), F32), cos, cos, pad], axis=1)
    sn = jnp.concatenate([jnp.zeros((n, MLA_NOPE), F32), sin, sin, pad], axis=1)
    return cs, sn


def _rotate_half_cols(w):
    half = MLA_ROPE // 2
    return jnp.concatenate([-w[..., half:], w[..., :half]], axis=-1)


def _mixer_in_weights(w_in, w_uq, w_ukv):
    d = w_in.shape[0]
    w_sbq = w_in[:, 0:512] * (64 ** -0.5 * LOG2_E)
    w_kr = w_in[:, 2176:2208]
    z_nope = jnp.zeros((d, MLA_NOPE), F32)
    z_pad = jnp.zeros((d, MLA_QK_PAD - MLA_NOPE - MLA_ROPE), F32)
    w_a = jnp.concatenate([w_sbq, w_in[:, 512:2176],
                           z_nope, w_kr, z_pad,
                           z_nope, _rotate_half_cols(w_kr), z_pad], axis=1)
    q_nope, q_rope = w_uq[:, :, :MLA_NOPE], w_uq[:, :, MLA_NOPE:]
    zq_nope = jnp.zeros_like(q_nope)
    zq_pad = jnp.zeros(q_rope.shape[:2] + (MLA_QK_PAD - MLA_NOPE - MLA_ROPE,), F32)
    w_q = jnp.concatenate([
        jnp.concatenate([q_nope, q_rope, zq_pad], -1).reshape(MLA_Q_LORA, -1),
        jnp.concatenate([zq_nope, _rotate_half_cols(q_rope), zq_pad], -1).reshape(MLA_Q_LORA, -1),
    ], axis=1)
    k_nope, v = w_ukv[:, :, :MLA_NOPE], w_ukv[:, :, MLA_NOPE:]
    w_kv = jnp.concatenate([
        jnp.concatenate([k_nope, jnp.zeros_like(k_nope)], -1).reshape(MLA_KV_LORA, -1),
        v.reshape(MLA_KV_LORA, -1)], axis=1)
    return _bf(w_a), _bf(w_q), _bf(w_kv)


def _pad_rows(a, rows):
    return jnp.pad(a, ((0, rows - a.shape[0]), (0, 0)))


def _layer(h, meta, p, tm, tq):
    bsz, seq, d = h.shape
    g = lambda v: v.reshape(1, -1)
    w_a, w_q, w_kv = _mixer_in_weights(p["w_in"], p["w_uq"], p["w_ukv"])
    w1 = [g(p["ffn1_pre_g"]), _bf(p["ffn1_w_gate"]), _bf(p["ffn1_w_up"]), _bf(p["ffn1_w_down"]),
          g(p["ffn1_post_g"]), g(p["mix_pre_g"]), w_a, g(p["q_norm_g"]), w_q,
          g(p["kv_norm_g"]), w_kv]
    cs, sn = _rope_tables(N_META + jnp.arange(seq))
    cs_m, sn_m = _rope_tables(jnp.arange(N_META))

    h1, sbq, sbk, sbv, qm, km, vm = _ffn_proj_call(h.reshape(bsz * seq, d), cs, sn, w1, tm)
    _, _, sbk_m, sbv_m, _, km_m, vm_m = _ffn_proj_call(meta, cs_m, sn_m, w1, N_META)

    ysb = _sb_call(sbq, sbk, sbv, _pad_rows(sbk_m, META_PAD), _pad_rows(sbv_m, META_PAD),
                   bsz, seq, tq)
    ymla = _mla_call(qm, km, vm, _pad_rows(km_m, META_PAD), _pad_rows(vm_m, META_PAD),
                     bsz, seq, tq)

    w4 = [g(p["mix_pre_g"]), _bf(p["w_in"][:, 2208:]), g(p["b_gate"]), _bf(p["w_sb_o"]),
          _bf(p["w_mla_o"]), _bf(p["w_out"]), g(p["mix_post_g"]), g(p["ffn2_pre_g"]),
          _bf(p["ffn2_w_gate"]), _bf(p["ffn2_w_up"]), _bf(p["ffn2_w_down"]),
          g(p["ffn2_post_g"])]
    out = _merge_ffn_call(h1, ysb, ymla, w4, tm)
    return out.reshape(bsz, seq, d)


def kernel(x, meta_tokens, ffn1_pre_g, ffn1_w_gate, ffn1_w_up, ffn1_w_down, ffn1_post_g, mix_pre_g, w_in, b_gate, q_norm_g, w_uq, kv_norm_g, w_ukv, w_sb_o, w_mla_o, w_out, mix_post_g, ffn2_pre_g, ffn2_w_gate, ffn2_w_up, ffn2_w_down, ffn2_post_g):
    names = ("ffn1_pre_g", "ffn1_w_gate", "ffn1_w_up", "ffn1_w_down", "ffn1_post_g", "mix_pre_g",
             "w_in", "b_gate", "q_norm_g", "w_uq", "kv_norm_g", "w_ukv", "w_sb_o", "w_mla_o",
             "w_out", "mix_post_g", "ffn2_pre_g", "ffn2_w_gate", "ffn2_w_up", "ffn2_w_down",
             "ffn2_post_g")
    stacked = (ffn1_pre_g, ffn1_w_gate, ffn1_w_up, ffn1_w_down, ffn1_post_g, mix_pre_g, w_in,
               b_gate, q_norm_g, w_uq, kv_norm_g, w_ukv, w_sb_o, w_mla_o, w_out, mix_post_g,
               ffn2_pre_g, ffn2_w_gate, ffn2_w_up, ffn2_w_down, ffn2_post_g)
    assert all(w.shape[0] == 1 for w in stacked), "single-layer block"
    seq = x.shape[1]
    tq = 2 * SB_TK
    tm = 512
    assert seq % tq == 0 and (x.shape[0] * seq) % tm == 0 and seq % tm == 0
    p = {n: w[0] for n, w in zip(names, stacked)}
    return _layer(x, meta_tokens.astype(x.dtype), p, tm, tq)
```

```python
import functools

import jax
import jax.numpy as jnp
from jax import lax
from jax.experimental import pallas as pl
from jax.experimental.pallas import tpu as pltpu

D_MODEL = 1024
D_FF = 2816
N_META = 16
CHUNK = 64
SB_WIDTH = 512
MLA_HEADS = 8
MLA_Q_LORA = 384
MLA_KV_LORA = 256
MLA_NOPE = 64
MLA_ROPE = 32
MLA_V = 64
ROPE_THETA = 10000.0
EPS = 1e-6
NEG_INF = -1e30
LOG2_E = 1.4426950408889634
HEAD_PAIR = 128
N_PAIRS = SB_WIDTH // HEAD_PAIR
MLA_QK_PAD = 128
META_PAD = 128
SB_TK = 256
ATT_TQ = 2 * SB_TK
ROW_GROUPS = 2
MIN_GROUP_ROWS = 128
DENSE_TM = 512
FINE_FFN_CHUNKS = 11
DENSE_STEPS = 2
VMEM_LIMIT = 60 * 1024 * 1024

F32 = jnp.float32
BF16 = jnp.bfloat16


def _bf(x):
    return x.astype(BF16)


def _dot(a, b):
    return jnp.dot(a, b, preferred_element_type=F32)


def _dot_nt(a, b):
    return lax.dot_general(a, b, (((1,), (1,)), ((), ())), preferred_element_type=F32)


def _rms(x, g):
    return x * lax.rsqrt(jnp.mean(x * x, axis=-1, keepdims=True) + EPS) * g


def _row_stat(x):
    return jnp.broadcast_to(x, (x.shape[0], HEAD_PAIR))


def _across_lanes(stat, width):
    return stat if width == HEAD_PAIR else jnp.concatenate([stat] * (width // HEAD_PAIR), axis=1)


def _trace_pipelined(tiles, background=None):
    done = [False] * len(tiles)
    background_live = background is not None
    step = 0
    while not all(done) or background_live:
        if background_live:
            try:
                next(background)
            except StopIteration:
                background_live = False
        for g, (gen, on_done) in enumerate(tiles):
            if step >= g and not done[g]:
                try:
                    next(gen)
                except StopIteration as stop:
                    done[g] = True
                    on_done(stop.value)
        step += 1


def _trace_row_groups(n_rows, chain):
    rows = n_rows // ROW_GROUPS if n_rows % (ROW_GROUPS * MIN_GROUP_ROWS) == 0 else n_rows
    _trace_pipelined([(chain(slice(r, r + rows), False), lambda value: None)
                      for r in range(0, n_rows, rows)])


def _ffn(h, pre_g, wg_ref, wu_ref, wd_ref, post_g, chunks):
    xn = _bf(_rms(h, pre_g))
    yield
    width = D_FF // chunks
    f = None
    for c in range(chunks):
        cols = slice(c * width, (c + 1) * width)
        g = _dot(xn, wg_ref[:, cols])
        u = _dot(xn, wu_ref[:, cols])
        yield
        part = _dot(_bf(g * jax.nn.sigmoid(g) * u), wd_ref[cols, :])
        f = part if f is None else f + part
        yield
    return h + 0.5 * _rms(f, post_g)


def _proj_chain(ins, outs, rs, fine):
    x_ref, cs_ref, sn_ref, pre_g, wg, wu, wd, post_g, mix_g, w_a, qn_g, w_q, kvn_g, w_kv = ins
    h1_ref, sbq_ref, sbk_ref, sbv_ref, qm_ref, km_ref, vm_ref = outs
    scale = (MLA_NOPE + MLA_ROPE) ** -0.5 * LOG2_E
    half = MLA_HEADS * MLA_QK_PAD
    h1 = yield from _ffn(x_ref[rs, :], pre_g[...], wg, wu, wd, post_g[...],
                         FINE_FFN_CHUNKS if fine else 1)
    h1_ref[rs, :] = h1
    u = _bf(_rms(h1, mix_g[...]))
    yield
    bounds = (0, 512, 1024, 1536, w_a.shape[1]) if fine else (0, w_a.shape[1])
    pieces = []
    for lo, hi in zip(bounds, bounds[1:]):
        pieces.append(_dot(u, w_a[:, lo:hi]))
        if fine:
            yield
    p = pieces[0] if len(pieces) == 1 else jnp.concatenate(pieces, axis=1)
    sbq_ref[rs, :] = _bf(p[:, 0:512])
    sbk_ref[rs, :] = _bf(p[:, 512:1024])
    sbv_ref[rs, :] = _bf(p[:, 1024:1536])
    c_q = _bf(_rms(p[:, 1536:1920], qn_g[...]))
    c_kv = _bf(_rms(p[:, 1920:2176], kvn_g[...]))
    kr = p[:, 2176:2304]
    kr_rot = p[:, 2304:2432]
    yield
    q2 = _dot(c_q, w_q[...])
    if fine:
        yield
    kv = _dot(c_kv, w_kv[...])
    yield
    cs = cs_ref[rs, :]
    sn = sn_ref[rs, :]
    k_rope = kr * cs + kr_rot * sn
    for h in range(MLA_HEADS):
        lo, hi = h * MLA_QK_PAD, (h + 1) * MLA_QK_PAD
        qm_ref[rs, lo:hi] = _bf((q2[:, lo:hi] * cs + q2[:, half + lo:half + hi] * sn) * scale)
        km_ref[rs, lo:hi] = _bf(kv[:, lo:hi] + k_rope)
    vm_ref[rs, :] = _bf(kv[:, half:])


def _merge_chain(ins, outs, rs, fine):
    (h1_ref, ysb_ref, ymla_ref, mix_g, w_gate, b_gate, w_sbo, w_mlao, w_out, mixp_g, pre_g,
     wg, wu, wd, post_g) = ins
    (o_ref,) = outs
    h1 = h1_ref[rs, :]
    u = _bf(_rms(h1, mix_g[...]))
    yield
    gated = []
    for branch, (y_ref, w_o) in enumerate(((ysb_ref, w_sbo), (ymla_ref, w_mlao))):
        cols = slice(branch * D_MODEL, (branch + 1) * D_MODEL)
        gate = jax.nn.sigmoid(_dot(u, w_gate[:, cols]) + b_gate[:, cols])
        gated.append(gate * _dot(y_ref[rs, :], w_o[...]))
        if fine:
            yield
    m = _dot(_bf(gated[0] + gated[1]), w_out[...])
    yield
    h2 = h1 + _rms(m, mixp_g[...])
    o_ref[rs, :] = yield from _ffn(h2, pre_g[...], wg, wu, wd, post_g[...],
                                   FINE_FFN_CHUNKS if fine else 1)


PROJ_WIDTHS = (D_MODEL, SB_WIDTH, SB_WIDTH, SB_WIDTH, MLA_HEADS * MLA_QK_PAD,
               MLA_HEADS * MLA_QK_PAD, MLA_HEADS * MLA_V)
PROJ_DTYPES = (F32,) + (BF16,) * 6
N_PROJ_IN = 14
N_MERGE_IN = 15


def _const_spec(shape):
    return pl.BlockSpec(shape, lambda *_: (0,) * len(shape), pipeline_mode=pl.Buffered(1))


def _sb_tile(qh, load_kv, ntri, mask):
    kblk, vblk = load_kv()
    z2 = _dot_nt(qh, kblk)
    sp = jnp.maximum(z2, 0.0) + jnp.log2(1.0 + jnp.exp2(-jnp.abs(z2)))
    log2_b = z2 - sp
    if mask is not None:
        sp = jnp.where(mask, sp, 0.0)
    sp_sum = _row_stat(jnp.sum(sp, axis=-1, keepdims=True))
    sp = _bf(sp)
    yield
    a = jnp.exp2(log2_b + _dot(sp, ntri))
    if mask is not None:
        a = jnp.where(mask, a, 0.0)
    a = _bf(a)
    yield
    return _dot(a, vblk), sp_sum


def _sb_tiles(n_blocks, q_ref, k_ref, v_ref, mk_ref, mv_ref, ntri_ref, o_ref):
    tq, tk = ATT_TQ, SB_TK
    start = n_blocks * tq
    lane = lax.broadcasted_iota(jnp.int32, (tq, HEAD_PAIR), 1)
    row = lax.broadcasted_iota(jnp.int32, (tq, tk), 0)
    col = lax.broadcasted_iota(jnp.int32, (tq, tk), 1)
    far_mask = (row >= tk) | (col < row)
    near_mask = lax.broadcasted_iota(jnp.int32, (tk, tk), 1) < lax.broadcasted_iota(
        jnp.int32, (tk, tk), 0)
    meta_mask = lax.broadcasted_iota(jnp.int32, (tq, META_PAD), 1) < N_META
    q = q_ref[...]
    ntri = ntri_ref[...]
    zero = jnp.zeros_like(q)
    qs = [jnp.where(lane < 64, q, zero), jnp.where(lane >= 64, q, zero)]

    def kv(s):
        return lambda: (k_ref[pl.ds(s, tk), :], v_ref[pl.ds(s, tk), :])

    def sweep(qh):
        yield _sb_tile(qh[tk:], kv(start + tk), ntri, near_mask)
        yield _sb_tile(qh, kv(start), ntri, far_mask)
        for s in range(start - tk, -1, -tk):
            yield _sb_tile(qh, kv(s), ntri, None)
        yield _sb_tile(qh, lambda: (mk_ref[...], mv_ref[...]), ntri[:META_PAD, :META_PAD],
                       meta_mask)

    states = [None, None]

    def add(hh, result):
        pv, sp_sum = result
        if states[hh] is None:
            top = jnp.zeros((tk, HEAD_PAIR), F32)
            states[hh] = (jnp.concatenate([top, pv], axis=0),
                          jnp.concatenate([top, -sp_sum], axis=0))
        else:
            acc, log2_gap = states[hh]
            states[hh] = (acc + jnp.exp2(log2_gap) * pv, log2_gap - sp_sum)

    tiles = [(gen, functools.partial(add, hh))
             for pair in zip(sweep(qs[0]), sweep(qs[1])) for hh, gen in enumerate(pair)]

    def finish():
        o_ref[...] = _bf(jnp.where(lane < 64, states[0][0], states[1][0]))

    return tiles, finish


def _softmax_step(q, load_blocks, read_state):
    blocks = load_blocks()
    scores = []
    for kblk, _, mask in blocks:
        s = _dot_nt(q, kblk)
        scores.append(s if mask is None else jnp.where(mask, s, NEG_INF))
    m_new = functools.reduce(jnp.maximum, [_row_stat(jnp.max(s, axis=-1, keepdims=True))
                                           for s in scores])
    yield
    state = read_state()
    if state is not None:
        m, l, acc = state
        m_new = jnp.maximum(m, m_new)
        alpha = jnp.exp2(m - m_new)
    probs = [jnp.exp2(s - _across_lanes(m_new, s.shape[1])) for s in scores]
    l_terms = [_row_stat(jnp.sum(p, axis=-1, keepdims=True)) for p in probs]
    probs = [_bf(p) for p in probs]
    yield
    pv_terms = [_dot(p, vblk) for p, (_, vblk, _) in zip(probs, blocks)]
    if state is not None:
        l_terms.insert(0, alpha * l)
        pv_terms.insert(0, alpha * acc)
    return (m_new, functools.reduce(lambda a, b: a + b, l_terms),
            functools.reduce(lambda a, b: a + b, pv_terms))


def _mla_tiles(n_blocks, q_ref, k_ref, v_ref, mk_ref, mv_ref, o_ref):
    tq = ATT_TQ
    half = tq // 2
    start = n_blocks * tq
    lane = lax.broadcasted_iota(jnp.int32, (tq, HEAD_PAIR), 1)
    row = lax.broadcasted_iota(jnp.int32, (half, tq), 0)
    col = lax.broadcasted_iota(jnp.int32, (half, tq), 1)
    top_mask = ((col // CHUNK) <= (row // CHUNK))[:, :half]
    bot_mask = (col // CHUNK) <= ((row + half) // CHUNK)
    meta_mask = lax.broadcasted_iota(jnp.int32, (half, META_PAD), 1) < N_META
    lanes = [(hh * MLA_QK_PAD, (hh + 1) * MLA_QK_PAD) for hh in range(2)]
    qs = [q_ref[:, lo:hi] for lo, hi in lanes]
    states = [{}, {}]

    def diag(hh, part, rows, keys, mask):
        lo, hi = lanes[hh]
        load = lambda: [(mk_ref[:, lo:hi], mv_ref[...], meta_mask),
                        (k_ref[pl.ds(start, keys), lo:hi], v_ref[pl.ds(start, keys), :], mask)]
        return part, _softmax_step(qs[hh][rows], load, lambda: None)

    def joined(hh):
        return tuple(jnp.concatenate([t, b], axis=0)
                     for t, b in zip(states[hh]["top"], states[hh]["bot"]))

    def block(hh, j):
        lo, hi = lanes[hh]
        load = lambda: [(k_ref[pl.ds(j * tq, tq), lo:hi], v_ref[pl.ds(j * tq, tq), :], None)]
        read = (lambda: joined(hh)) if j == 0 else (lambda: states[hh]["all"])
        return "all", _softmax_step(qs[hh], load, read)

    per_head = [[diag(hh, "top", slice(0, half), half, top_mask),
                 diag(hh, "bot", slice(half, tq), tq, bot_mask)]
                + [block(hh, j) for j in range(n_blocks)] for hh in range(2)]

    def store(hh, part, state):
        states[hh][part] = state

    tiles = [(gen, functools.partial(store, hh, part))
             for pair in zip(*per_head) for hh, (part, gen) in enumerate(pair)]

    def finish():
        outs = []
        for hh in range(2):
            _, l, acc = states[hh]["all"] if n_blocks else joined(hh)
            outs.append(acc / l)
        o_ref[...] = _bf(jnp.where(lane < 64, outs[0], outs[1]))

    return tiles, finish


N_ATT_IN = 11


def _attention_specs(seq):
    nq = seq // ATT_TQ
    pair = 2 * MLA_QK_PAD
    tile = lambda w: pl.BlockSpec((ATT_TQ, w), lambda b, h, i: (b * nq + i, h))
    keys = lambda w: pl.BlockSpec((seq, w), lambda b, h, i: (b, h))
    meta = lambda w: pl.BlockSpec((META_PAD, w), lambda b, h, i: (0, h))
    in_specs = [tile(HEAD_PAIR), keys(HEAD_PAIR), keys(HEAD_PAIR), meta(HEAD_PAIR),
                meta(HEAD_PAIR), pl.BlockSpec((SB_TK, SB_TK), lambda b, h, i: (0, 0)),
                tile(pair), keys(pair), keys(HEAD_PAIR), meta(pair), meta(HEAD_PAIR)]
    return in_specs, [tile(HEAD_PAIR), tile(HEAD_PAIR)]


def _proj_kernel(*refs):
    ins, outs = refs[:N_PROJ_IN], refs[N_PROJ_IN:]
    _trace_row_groups(ins[0].shape[0], functools.partial(_proj_chain, ins, outs))


def _proj_call(x2d, cs, sn, weights, n_rows, tm):
    n_pos_blocks = cs.shape[0] // tm
    row = lambda w: pl.BlockSpec((tm, w), lambda i: (i, 0))
    pos = pl.BlockSpec((tm, MLA_QK_PAD), lambda i: (i % n_pos_blocks, 0))
    return pl.pallas_call(
        _proj_kernel,
        grid=(n_rows // tm,),
        in_specs=[row(D_MODEL), pos, pos] + [_const_spec(w.shape) for w in weights],
        out_specs=[row(w) for w in PROJ_WIDTHS],
        out_shape=[jax.ShapeDtypeStruct((n_rows, w), d)
                   for w, d in zip(PROJ_WIDTHS, PROJ_DTYPES)],
        compiler_params=pltpu.CompilerParams(dimension_semantics=("parallel",),
                                             vmem_limit_bytes=VMEM_LIMIT),
        name="ffn1_proj",
    )(x2d, cs, sn, *weights)


def _merge_kernel(*refs):
    ins, outs = refs[:N_MERGE_IN], refs[N_MERGE_IN + 1:]
    _trace_row_groups(ins[0].shape[0], functools.partial(_merge_chain, ins, outs))


def _merge_call(h1, ysb, ymla, weights, out_buf, first_block, tm):
    row = lambda w: pl.BlockSpec((tm, w), lambda i: (i, 0))
    return pl.pallas_call(
        _merge_kernel,
        grid=(h1.shape[0] // tm,),
        in_specs=[row(D_MODEL), row(SB_WIDTH), row(SB_WIDTH)]
                 + [_const_spec(w.shape) for w in weights] + [pl.BlockSpec(memory_space=pl.ANY)],
        out_specs=pl.BlockSpec((tm, D_MODEL), lambda i: (first_block + i, 0)),
        out_shape=jax.ShapeDtypeStruct(out_buf.shape, F32),
        input_output_aliases={N_MERGE_IN: 0},
        compiler_params=pltpu.CompilerParams(dimension_semantics=("parallel",),
                                             vmem_limit_bytes=VMEM_LIMIT),
        name="merge_ffn2",
    )(h1, ysb, ymla, *weights, out_buf)


def _shared_kernel(*refs, nq, chain, n_dense_in):
    sb_in = refs[0:6]
    mla_in = refs[6:N_ATT_IN]
    dense_in = refs[N_ATT_IN:N_ATT_IN + n_dense_in]
    ysb_ref, ymla_ref = refs[N_ATT_IN + n_dense_in:N_ATT_IN + n_dense_in + 2]
    dense_out = refs[N_ATT_IN + n_dense_in + 2:]

    def program(n):
        sb_tiles, sb_finish = _sb_tiles(n, *sb_in, ysb_ref)
        mla_tiles, mla_finish = _mla_tiles(n, *mla_in, ymla_ref)
        rows = slice(0, dense_in[0].shape[0])
        dense = chain(dense_in, dense_out, rows, True) if n >= nq - DENSE_STEPS else None
        _trace_pipelined(sb_tiles + mla_tiles, dense)
        sb_finish()
        mla_finish()

    for n in range(nq):
        pl.when(pl.program_id(2) == n)(functools.partial(program, n))


def _shared_call(att_in, seq, n_batch, chain, dense_in, dense_in_specs, dense_out_specs,
                 dense_out_shape, name):
    nq = seq // ATT_TQ
    att_in_specs, att_out_specs = _attention_specs(seq)
    att_shape = jax.ShapeDtypeStruct((n_batch * seq, SB_WIDTH), BF16)
    return pl.pallas_call(
        functools.partial(_shared_kernel, nq=nq, chain=chain, n_dense_in=len(dense_in)),
        grid=(n_batch, N_PAIRS, nq),
        in_specs=att_in_specs + dense_in_specs,
        out_specs=att_out_specs + dense_out_specs,
        out_shape=[att_shape, att_shape] + dense_out_shape,
        compiler_params=pltpu.CompilerParams(
            dimension_semantics=("parallel", "parallel", "arbitrary"),
            vmem_limit_bytes=VMEM_LIMIT),
        name=name,
    )(*att_in, *dense_in)


def _dense_block_map(seq, first_batch):
    nq = seq // ATT_TQ
    per_batch = N_PAIRS * DENSE_STEPS

    def in_batch(h, i):
        return h * DENSE_STEPS + jnp.maximum(i - (nq - DENSE_STEPS), 0)

    return (lambda b, h, i: ((first_batch + b) * per_batch + in_batch(h, i), 0),
            lambda b, h, i: (in_batch(h, i), 0))


def _rope_tables(pos):
    half = MLA_ROPE // 2
    inv = ROPE_THETA ** (-jnp.arange(half, dtype=F32) / half)
    ang = pos.astype(F32)[:, None] * inv[None, :]
    cos, sin = jnp.cos(ang), jnp.sin(ang)
    n = pos.shape[0]
    pad = jnp.zeros((n, MLA_QK_PAD - MLA_NOPE - MLA_ROPE), F32)
    cs = jnp.concatenate([jnp.ones((n, MLA_NOPE), F32), cos, cos, pad], axis=1)
    sn = jnp.concatenate([jnp.zeros((n, MLA_NOPE), F32), sin, sin, pad], axis=1)
    return cs, sn


def _rotate_half_cols(w):
    half = MLA_ROPE // 2
    return jnp.concatenate([-w[..., half:], w[..., :half]], axis=-1)


def _mixer_in_weights(w_in, w_uq, w_ukv):
    d = w_in.shape[0]
    w_sbq = w_in[:, 0:512] * (64 ** -0.5 * LOG2_E)
    w_kr = w_in[:, 2176:2208]
    z_nope = jnp.zeros((d, MLA_NOPE), F32)
    z_pad = jnp.zeros((d, MLA_QK_PAD - MLA_NOPE - MLA_ROPE), F32)
    w_a = jnp.concatenate([w_sbq, w_in[:, 512:2176],
                           z_nope, w_kr, z_pad,
                           z_nope, _rotate_half_cols(w_kr), z_pad], axis=1)
    q_nope, q_rope = w_uq[:, :, :MLA_NOPE], w_uq[:, :, MLA_NOPE:]
    zq_nope = jnp.zeros_like(q_nope)
    zq_pad = jnp.zeros(q_rope.shape[:2] + (MLA_QK_PAD - MLA_NOPE - MLA_ROPE,), F32)
    w_q = jnp.concatenate([
        jnp.concatenate([q_nope, q_rope, zq_pad], -1).reshape(MLA_Q_LORA, -1),
        jnp.concatenate([zq_nope, _rotate_half_cols(q_rope), zq_pad], -1).reshape(MLA_Q_LORA, -1),
    ], axis=1)
    k_nope, v = w_ukv[:, :, :MLA_NOPE], w_ukv[:, :, MLA_NOPE:]
    w_kv = jnp.concatenate([
        jnp.concatenate([k_nope, jnp.zeros_like(k_nope)], -1).reshape(MLA_KV_LORA, -1),
        v.reshape(MLA_KV_LORA, -1)], axis=1)
    return _bf(w_a), _bf(w_q), _bf(w_kv)


def _pad_rows(a, rows):
    return jnp.pad(a, ((0, rows - a.shape[0]), (0, 0)))


def _layer(h, meta, p):
    bsz, seq, d = h.shape
    half = bsz // 2
    rows_half = half * seq
    dense_rows = seq // (N_PAIRS * DENSE_STEPS)
    g = lambda v: v.reshape(1, -1)
    w_a, w_q, w_kv = _mixer_in_weights(p["w_in"], p["w_uq"], p["w_ukv"])
    w1 = [g(p["ffn1_pre_g"]), _bf(p["ffn1_w_gate"]), _bf(p["ffn1_w_up"]), _bf(p["ffn1_w_down"]),
          g(p["ffn1_post_g"]), g(p["mix_pre_g"]), w_a, g(p["q_norm_g"]), w_q,
          g(p["kv_norm_g"]), w_kv]
    w4 = [g(p["mix_pre_g"]), _bf(p["w_in"][:, 2208:]), g(p["b_gate"]), _bf(p["w_sb_o"]),
          _bf(p["w_mla_o"]), _bf(p["w_out"]), g(p["mix_post_g"]), g(p["ffn2_pre_g"]),
          _bf(p["ffn2_w_gate"]), _bf(p["ffn2_w_up"]), _bf(p["ffn2_w_down"]),
          g(p["ffn2_post_g"])]
    cs, sn = _rope_tables(N_META + jnp.arange(seq))
    cs_m, sn_m = _rope_tables(jnp.arange(N_META))
    x2d = h.reshape(bsz * seq, d)
    ntri = -(lax.broadcasted_iota(jnp.int32, (SB_TK, SB_TK), 0)
             > lax.broadcasted_iota(jnp.int32, (SB_TK, SB_TK), 1)).astype(BF16)

    _, _, sbk_m, sbv_m, _, km_m, vm_m = _proj_call(meta, cs_m, sn_m, w1, N_META, N_META)
    meta_kv = [_pad_rows(a, META_PAD) for a in (sbk_m, sbv_m, km_m, vm_m)]

    def attention_inputs(proj):
        _, sbq, sbk, sbv, qm, km, vm = proj
        return [sbq, sbk, sbv, meta_kv[0], meta_kv[1], ntri, qm, km, vm, meta_kv[2], meta_kv[3]]

    block = lambda w, index_map: pl.BlockSpec((dense_rows, w), index_map)
    weight_specs = lambda ws: [_const_spec(w.shape) for w in ws]

    proj0 = _proj_call(x2d, cs, sn, w1, rows_half, DENSE_TM)

    x_map, pos_map = _dense_block_map(seq, half)
    out_map, _ = _dense_block_map(seq, 0)
    ysb0, ymla0, *proj1 = _shared_call(
        attention_inputs(proj0), seq, half, _proj_chain, [x2d, cs, sn] + w1,
        [block(d, x_map), block(MLA_QK_PAD, pos_map), block(MLA_QK_PAD, pos_map)]
        + weight_specs(w1),
        [block(w, out_map) for w in PROJ_WIDTHS],
        [jax.ShapeDtypeStruct((rows_half, w), dt) for w, dt in zip(PROJ_WIDTHS, PROJ_DTYPES)],
        "attention_ffn1_proj")

    ysb1, ymla1, out = _shared_call(
        attention_inputs(proj1), seq, half, _merge_chain, [proj0[0], ysb0, ymla0] + w4,
        [block(d, out_map), block(SB_WIDTH, out_map), block(SB_WIDTH, out_map)]
        + weight_specs(w4),
        [block(d, out_map)], [jax.ShapeDtypeStruct((bsz * seq, d), F32)],
        "attention_merge_ffn2")

    out = _merge_call(proj1[0], ysb1, ymla1, w4, out, rows_half // DENSE_TM, DENSE_TM)
    return out.reshape(bsz, seq, d)


def kernel(x, meta_tokens, ffn1_pre_g, ffn1_w_gate, ffn1_w_up, ffn1_w_down, ffn1_post_g, mix_pre_g, w_in, b_gate, q_norm_g, w_uq, kv_norm_g, w_ukv, w_sb_o, w_mla_o, w_out, mix_post_g, ffn2_pre_g, ffn2_w_gate, ffn2_w_up, ffn2_w_down, ffn2_post_g):
    names = ("ffn1_pre_g", "ffn1_w_gate", "ffn1_w_up", "ffn1_w_down", "ffn1_post_g", "mix_pre_g",
             "w_in", "b_gate", "q_norm_g", "w_uq", "kv_norm_g", "w_ukv", "w_sb_o", "w_mla_o",
             "w_out", "mix_post_g", "ffn2_pre_g", "ffn2_w_gate", "ffn2_w_up", "ffn2_w_down",
             "ffn2_post_g")
    stacked = (ffn1_pre_g, ffn1_w_gate, ffn1_w_up, ffn1_w_down, ffn1_post_g, mix_pre_g, w_in,
               b_gate, q_norm_g, w_uq, kv_norm_g, w_ukv, w_sb_o, w_mla_o, w_out, mix_post_g,
               ffn2_pre_g, ffn2_w_gate, ffn2_w_up, ffn2_w_down, ffn2_post_g)
    assert all(w.shape[0] == 1 for w in stacked), "single-layer block"
    bsz, seq = x.shape[:2]
    assert bsz % 2 == 0 and seq % ATT_TQ == 0 and seq // ATT_TQ >= DENSE_STEPS
    assert (bsz // 2 * seq) % DENSE_TM == 0 and seq % DENSE_TM == 0
    assert seq % (N_PAIRS * DENSE_STEPS * MIN_GROUP_ROWS) == 0
    p = {n: w[0] for n, w in zip(names, stacked)}
    return _layer(x, meta_tokens.astype(x.dtype), p)
```

```python
import functools

import jax
import jax.numpy as jnp
from jax import lax
from jax.experimental import pallas as pl
from jax.experimental.pallas import tpu as pltpu

D_MODEL = 1024
D_FF = 2816
N_META = 16
CHUNK = 64
SB_WIDTH = 512
MLA_HEADS = 8
MLA_Q_LORA = 384
MLA_KV_LORA = 256
MLA_NOPE = 64
MLA_ROPE = 32
MLA_V = 64
ROPE_THETA = 10000.0
EPS = 1e-6
NEG_INF = -1e30
LOG2_E = 1.4426950408889634
HEAD_PAIR = 128
MLA_QK_PAD = 128
META_PAD = 128
SB_TK = 256
ROW_GROUPS = 2
MIN_GROUP_ROWS = 128
VMEM_LIMIT = 56 * 1024 * 1024

F32 = jnp.float32
BF16 = jnp.bfloat16


def _bf(x):
    return x.astype(BF16)


def _dot(a, b):
    return jnp.dot(a, b, preferred_element_type=F32)


def _dot_nt(a, b):
    return lax.dot_general(a, b, (((1,), (1,)), ((), ())), preferred_element_type=F32)


def _rms(x, g):
    return x * lax.rsqrt(jnp.mean(x * x, axis=-1, keepdims=True) + EPS) * g


def _row_stat(x):
    return jnp.broadcast_to(x, (x.shape[0], HEAD_PAIR))


def _across_lanes(stat, width):
    return stat if width == HEAD_PAIR else jnp.concatenate([stat] * (width // HEAD_PAIR), axis=1)


def _trace_pipelined(stage_gens, on_done):
    done = [False] * len(stage_gens)
    step = 0
    while not all(done):
        for g, gen in enumerate(stage_gens):
            if step >= g and not done[g]:
                try:
                    next(gen)
                except StopIteration as stop:
                    done[g] = True
                    on_done(g, stop.value)
        step += 1


def _ffn(h, pre_g, wg_ref, wu_ref, wd_ref, post_g):
    xn = _bf(_rms(h, pre_g))
    yield
    g = _dot(xn, wg_ref[...])
    u = _dot(xn, wu_ref[...])
    yield
    f = _dot(_bf(g * jax.nn.sigmoid(g) * u), wd_ref[...])
    yield
    return h + 0.5 * _rms(f, post_g)


def _trace_row_groups(n_rows, chain):
    rows = n_rows // ROW_GROUPS if n_rows % (ROW_GROUPS * MIN_GROUP_ROWS) == 0 else n_rows
    _trace_pipelined([chain(r, rows) for r in range(0, n_rows, rows)], lambda g, value: None)


def _ffn_proj_kernel(x_ref, cs_ref, sn_ref, pre_g, wg, wu, wd, post_g, mix_g, w_a, qn_g, w_q,
                     kvn_g, w_kv, h1_ref, sbq_ref, sbk_ref, sbv_ref, qm_ref, km_ref, vm_ref):
    scale = (MLA_NOPE + MLA_ROPE) ** -0.5 * LOG2_E
    half = MLA_HEADS * MLA_QK_PAD

    def chain(r, rows):
        rs = slice(r, r + rows)
        h1 = yield from _ffn(x_ref[rs, :], pre_g[...], wg, wu, wd, post_g[...])
        h1_ref[rs, :] = h1
        u = _bf(_rms(h1, mix_g[...]))
        yield
        p = _dot(u, w_a[...])
        sbq_ref[rs, :] = _bf(p[:, 0:512])
        sbk_ref[rs, :] = _bf(p[:, 512:1024])
        sbv_ref[rs, :] = _bf(p[:, 1024:1536])
        c_q = _bf(_rms(p[:, 1536:1920], qn_g[...]))
        c_kv = _bf(_rms(p[:, 1920:2176], kvn_g[...]))
        kr = p[:, 2176:2304]
        kr_rot = p[:, 2304:2432]
        yield
        q2 = _dot(c_q, w_q[...])
        kv = _dot(c_kv, w_kv[...])
        yield
        cs = cs_ref[rs, :]
        sn = sn_ref[rs, :]
        k_rope = kr * cs + kr_rot * sn
        for h in range(MLA_HEADS):
            lo, hi = h * MLA_QK_PAD, (h + 1) * MLA_QK_PAD
            qm_ref[rs, lo:hi] = _bf((q2[:, lo:hi] * cs + q2[:, half + lo:half + hi] * sn) * scale)
            km_ref[rs, lo:hi] = _bf(kv[:, lo:hi] + k_rope)
        vm_ref[rs, :] = _bf(kv[:, half:])

    _trace_row_groups(x_ref.shape[0], chain)


def _const_spec(shape):
    return pl.BlockSpec(shape, lambda *_: (0,) * len(shape), pipeline_mode=pl.Buffered(1))


def _ffn_proj_call(x2d, cs, sn, weights, tm):
    t = x2d.shape[0]
    n_pos_blocks = cs.shape[0] // tm
    row = lambda w: pl.BlockSpec((tm, w), lambda i: (i, 0))
    pos = pl.BlockSpec((tm, MLA_QK_PAD), lambda i: (i % n_pos_blocks, 0))
    in_specs = [row(D_MODEL), pos, pos] + [_const_spec(w.shape) for w in weights]
    widths = (D_MODEL, SB_WIDTH, SB_WIDTH, SB_WIDTH, MLA_HEADS * MLA_QK_PAD,
              MLA_HEADS * MLA_QK_PAD, MLA_HEADS * MLA_V)
    dtypes = (F32,) + (BF16,) * 6
    return pl.pallas_call(
        _ffn_proj_kernel,
        grid=(t // tm,),
        in_specs=in_specs,
        out_specs=[row(w) for w in widths],
        out_shape=[jax.ShapeDtypeStruct((t, w), d) for w, d in zip(widths, dtypes)],
        compiler_params=pltpu.CompilerParams(dimension_semantics=("parallel",),
                                             vmem_limit_bytes=VMEM_LIMIT),
        name="ffn1_proj",
    )(x2d, cs, sn, *weights)


def _sb_tile(qh, load_kv, ntri, mask):
    kblk, vblk = load_kv()
    z2 = _dot_nt(qh, kblk)
    sp = jnp.maximum(z2, 0.0) + jnp.log2(1.0 + jnp.exp2(-jnp.abs(z2)))
    log2_b = z2 - sp
    if mask is not None:
        sp = jnp.where(mask, sp, 0.0)
    sp_sum = _row_stat(jnp.sum(sp, axis=-1, keepdims=True))
    sp = _bf(sp)
    yield
    a = jnp.exp2(log2_b + _dot(sp, ntri))
    if mask is not None:
        a = jnp.where(mask, a, 0.0)
    a = _bf(a)
    yield
    return _dot(a, vblk), sp_sum


def _trace_query_tiles(query_tile, nq, tq, q_ref, o_ref):
    tiles, finishes = [], []
    for n in range(nq):
        rows = slice(n * tq, (n + 1) * tq)
        new_tiles, finish = query_tile(n, q_ref[rows, :])
        tiles += new_tiles
        finishes.append((rows, finish))
    _trace_pipelined([gen for gen, _ in tiles], lambda g, value: tiles[g][1](value))
    for rows, finish in finishes:
        o_ref[rows, :] = finish()


def _sb_kernel(q_ref, k_ref, v_ref, mk_ref, mv_ref, ntri_ref, o_ref, *, tq, nq):
    _trace_query_tiles(
        lambda n, q: _sb_query_tile(n, q, k_ref, v_ref, mk_ref, mv_ref, ntri_ref, tq),
        nq, tq, q_ref, o_ref)


def _sb_query_tile(n_blocks, q, k_ref, v_ref, mk_ref, mv_ref, ntri_ref, tq):
    tk = SB_TK
    assert tq == 2 * tk
    start = n_blocks * tq
    lane = lax.broadcasted_iota(jnp.int32, (tq, HEAD_PAIR), 1)
    row = lax.broadcasted_iota(jnp.int32, (tq, tk), 0)
    col = lax.broadcasted_iota(jnp.int32, (tq, tk), 1)
    far_mask = (row >= tk) | (col < row)
    near_mask = lax.broadcasted_iota(jnp.int32, (tk, tk), 1) < lax.broadcasted_iota(
        jnp.int32, (tk, tk), 0)
    meta_mask = lax.broadcasted_iota(jnp.int32, (tq, META_PAD), 1) < N_META
    ntri = ntri_ref[...]
    zero = jnp.zeros_like(q)
    qs = [jnp.where(lane < 64, q, zero), jnp.where(lane >= 64, q, zero)]

    def kv(s):
        return lambda: (k_ref[pl.ds(s, tk), :], v_ref[pl.ds(s, tk), :])

    def sweep(qh):
        yield _sb_tile(qh[tk:], kv(start + tk), ntri, near_mask)
        yield _sb_tile(qh, kv(start), ntri, far_mask)
        for s in range(start - tk, -1, -tk):
            yield _sb_tile(qh, kv(s), ntri, None)
        yield _sb_tile(qh, lambda: (mk_ref[...], mv_ref[...]), ntri[:META_PAD, :META_PAD],
                       meta_mask)

    states = [None, None]

    def add(hh, result):
        pv, sp_sum = result
        if states[hh] is None:
            top = jnp.zeros((tk, HEAD_PAIR), F32)
            states[hh] = (jnp.concatenate([top, pv], axis=0),
                          jnp.concatenate([top, -sp_sum], axis=0))
        else:
            acc, log2_gap = states[hh]
            states[hh] = (acc + jnp.exp2(log2_gap) * pv, log2_gap - sp_sum)

    tiles = [(gen, functools.partial(add, hh))
             for pair in zip(sweep(qs[0]), sweep(qs[1])) for hh, gen in enumerate(pair)]
    return tiles, lambda: _bf(jnp.where(lane < 64, states[0][0], states[1][0]))


def _sb_call(sbq, sbk, sbv, mk, mv, bsz, seq, tq):
    nq = seq // tq
    ntri = -(lax.broadcasted_iota(jnp.int32, (SB_TK, SB_TK), 0)
             > lax.broadcasted_iota(jnp.int32, (SB_TK, SB_TK), 1)).astype(BF16)
    seqspec = pl.BlockSpec((seq, HEAD_PAIR), lambda b, h: (b, h))
    mspec = pl.BlockSpec((META_PAD, HEAD_PAIR), lambda b, h: (0, h))
    return pl.pallas_call(
        functools.partial(_sb_kernel, tq=tq, nq=nq),
        grid=(bsz, SB_WIDTH // HEAD_PAIR),
        in_specs=[seqspec, seqspec, seqspec, mspec, mspec,
                  pl.BlockSpec((SB_TK, SB_TK), lambda b, h: (0, 0))],
        out_specs=seqspec,
        out_shape=jax.ShapeDtypeStruct(sbq.shape, BF16),
        compiler_params=pltpu.CompilerParams(
            dimension_semantics=("parallel", "parallel"),
            vmem_limit_bytes=VMEM_LIMIT),
        name="sb_attention",
    )(sbq, sbk, sbv, mk, mv, ntri)


def _softmax_step(q, load_blocks, read_state):
    blocks = load_blocks()
    scores = []
    for kblk, _, mask in blocks:
        s = _dot_nt(q, kblk)
        scores.append(s if mask is None else jnp.where(mask, s, NEG_INF))
    m_new = functools.reduce(jnp.maximum, [_row_stat(jnp.max(s, axis=-1, keepdims=True))
                                           for s in scores])
    yield
    state = read_state()
    if state is not None:
        m, l, acc = state
        m_new = jnp.maximum(m, m_new)
        alpha = jnp.exp2(m - m_new)
    probs = [jnp.exp2(s - _across_lanes(m_new, s.shape[1])) for s in scores]
    l_terms = [_row_stat(jnp.sum(p, axis=-1, keepdims=True)) for p in probs]
    probs = [_bf(p) for p in probs]
    yield
    pv_terms = [_dot(p, vblk) for p, (_, vblk, _) in zip(probs, blocks)]
    if state is not None:
        l_terms.insert(0, alpha * l)
        pv_terms.insert(0, alpha * acc)
    return (m_new, functools.reduce(lambda a, b: a + b, l_terms),
            functools.reduce(lambda a, b: a + b, pv_terms))


def _mla_kernel(q_ref, k_ref, v_ref, mk_ref, mv_ref, o_ref, *, tq, nq):
    _trace_query_tiles(lambda n, q: _mla_query_tile(n, q, k_ref, v_ref, mk_ref, mv_ref, tq),
                       nq, tq, q_ref, o_ref)


def _mla_query_tile(n_blocks, q, k_ref, v_ref, mk_ref, mv_ref, tq):
    half = tq // 2
    start = n_blocks * tq
    lane = lax.broadcasted_iota(jnp.int32, (tq, HEAD_PAIR), 1)
    row = lax.broadcasted_iota(jnp.int32, (half, tq), 0)
    col = lax.broadcasted_iota(jnp.int32, (half, tq), 1)
    top_mask = ((col // CHUNK) <= (row // CHUNK))[:, :half]
    bot_mask = (col // CHUNK) <= ((row + half) // CHUNK)
    meta_mask = lax.broadcasted_iota(jnp.int32, (half, META_PAD), 1) < N_META
    lanes = [(hh * MLA_QK_PAD, (hh + 1) * MLA_QK_PAD) for hh in range(2)]
    qs = [q[:, lo:hi] for lo, hi in lanes]

    states = [{}, {}]

    def diag(hh, part, rows, keys, mask):
        lo, hi = lanes[hh]
        load = lambda: [(mk_ref[:, lo:hi], mv_ref[...], meta_mask),
                        (k_ref[pl.ds(start, keys), lo:hi], v_ref[pl.ds(start, keys), :], mask)]
        return part, _softmax_step(qs[hh][rows], load, lambda: None)

    def joined(hh):
        return tuple(jnp.concatenate([t, b], axis=0)
                     for t, b in zip(states[hh]["top"], states[hh]["bot"]))

    def block(hh, j):
        lo, hi = lanes[hh]
        load = lambda: [(k_ref[pl.ds(j * tq, tq), lo:hi], v_ref[pl.ds(j * tq, tq), :], None)]
        read = (lambda: joined(hh)) if j == 0 else (lambda: states[hh]["all"])
        return "all", _softmax_step(qs[hh], load, read)

    per_head = [[diag(hh, "top", slice(0, half), half, top_mask),
                 diag(hh, "bot", slice(half, tq), tq, bot_mask)]
                + [block(hh, j) for j in range(n_blocks)] for hh in range(2)]

    def store(hh, part, state):
        states[hh][part] = state

    tiles = [(gen, functools.partial(store, hh, part))
             for pair in zip(*per_head) for hh, (part, gen) in enumerate(pair)]

    def finish():
        outs = []
        for hh in range(2):
            _, l, acc = states[hh]["all"] if n_blocks else joined(hh)
            outs.append(acc / l)
        return _bf(jnp.where(lane < 64, outs[0], outs[1]))

    return tiles, finish


def _mla_call(qm, km, vm, mk, mv, bsz, seq, tq):
    nq = seq // tq
    pair = 2 * MLA_QK_PAD
    return pl.pallas_call(
        functools.partial(_mla_kernel, tq=tq, nq=nq),
        grid=(bsz, MLA_HEADS // 2),
        in_specs=[pl.BlockSpec((seq, pair), lambda b, h: (b, h)),
                  pl.BlockSpec((seq, pair), lambda b, h: (b, h)),
                  pl.BlockSpec((seq, HEAD_PAIR), lambda b, h: (b, h)),
                  pl.BlockSpec((META_PAD, pair), lambda b, h: (0, h)),
                  pl.BlockSpec((META_PAD, HEAD_PAIR), lambda b, h: (0, h))],
        out_specs=pl.BlockSpec((seq, HEAD_PAIR), lambda b, h: (b, h)),
        out_shape=jax.ShapeDtypeStruct(vm.shape, BF16),
        compiler_params=pltpu.CompilerParams(
            dimension_semantics=("parallel", "parallel"),
            vmem_limit_bytes=VMEM_LIMIT),
        name="mla_attention",
    )(qm, km, vm, mk, mv)


def _merge_ffn_kernel(h1_ref, ysb_ref, ymla_ref, mix_g, w_gate, b_gate, w_sbo, w_mlao, w_out,
                      mixp_g, pre_g, wg, wu, wd, post_g, o_ref):
    def chain(r, rows):
        rs = slice(r, r + rows)
        h1 = h1_ref[rs, :]
        u = _bf(_rms(h1, mix_g[...]))
        yield
        gate = jax.nn.sigmoid(_dot(u, w_gate[...]) + b_gate[...])
        y_sb = _dot(ysb_ref[rs, :], w_sbo[...])
        y_mla = _dot(ymla_ref[rs, :], w_mlao[...])
        yield
        merged = gate[:, :D_MODEL] * y_sb + gate[:, D_MODEL:] * y_mla
        m = _dot(_bf(merged), w_out[...])
        yield
        h2 = h1 + _rms(m, mixp_g[...])
        o_ref[rs, :] = yield from _ffn(h2, pre_g[...], wg, wu, wd, post_g[...])

    _trace_row_groups(h1_ref.shape[0], chain)


def _merge_ffn_call(h1, ysb, ymla, weights, tm):
    t = h1.shape[0]
    row = lambda w: pl.BlockSpec((tm, w), lambda i: (i, 0))
    return pl.pallas_call(
        _merge_ffn_kernel,
        grid=(t // tm,),
        in_specs=[row(D_MODEL), row(SB_WIDTH), row(SB_WIDTH)]
                 + [_const_spec(w.shape) for w in weights],
        out_specs=row(D_MODEL),
        out_shape=jax.ShapeDtypeStruct((t, D_MODEL), F32),
        compiler_params=pltpu.CompilerParams(dimension_semantics=("parallel",),
                                             vmem_limit_bytes=VMEM_LIMIT),
        name="merge_ffn2",
    )(h1, ysb, ymla, *weights)


def _rope_tables(pos):
    half = MLA_ROPE // 2
    inv = ROPE_THETA ** (-jnp.arange(half, dtype=F32) / half)
    ang = pos.astype(F32)[:, None] * inv[None, :]
    cos, sin = jnp.cos(ang), jnp.sin(ang)
    n = pos.shape[0]
    pad = jnp.zeros((n, MLA_QK_PAD - MLA_NOPE - MLA_ROPE), F32)
    cs = jnp.concatenate([jnp.ones((n, MLA_NOPE), F32), cos, cos, pad], axis=1)
    sn = jnp.concatenate([jnp.zeros((n, MLA_NOPE), F32), sin, sin, pad], axis=1)
    return cs, sn


def _rotate_half_cols(w):
    half = MLA_ROPE // 2
    return jnp.concatenate([-w[..., half:], w[..., :half]], axis=-1)


def _mixer_in_weights(w_in, w_uq, w_ukv):
    d = w_in.shape[0]
    w_sbq = w_in[:, 0:512] * (64 ** -0.5 * LOG2_E)
    w_kr = w_in[:, 2176:2208]
    z_nope = jnp.zeros((d, MLA_NOPE), F32)
    z_pad = jnp.zeros((d, MLA_QK_PAD - MLA_NOPE - MLA_ROPE), F32)
    w_a = jnp.concatenate([w_sbq, w_in[:, 512:2176],
                           z_nope, w_kr, z_pad,
                           z_nope, _rotate_half_cols(w_kr), z_pad], axis=1)
    q_nope, q_rope = w_uq[:, :, :MLA_NOPE], w_uq[:, :, MLA_NOPE:]
    zq_nope = jnp.zeros_like(q_nope)
    zq_pad = jnp.zeros(q_rope.shape[:2] + (MLA_QK_PAD - MLA_NOPE - MLA_ROPE,), F32)
    w_q = jnp.concatenate([
        jnp.concatenate([q_nope, q_rope, zq_pad], -1).reshape(MLA_Q_LORA, -1),
        jnp.concatenate([zq_nope, _rotate_half_cols(q_rope), zq_pad], -1).reshape(MLA_Q_LORA, -1),
    ], axis=1)
    k_nope, v = w_ukv[:, :, :MLA_NOPE], w_ukv[:, :, MLA_NOPE:]
    w_kv = jnp.concatenate([
        jnp.concatenate([k_nope, jnp.zeros_like(k_nope)], -1).reshape(MLA_KV_LORA, -1),
        v.reshape(MLA_KV_LORA, -1)], axis=1)
    return _bf(w_a), _bf(w_q), _bf(w_kv)


def _pad_rows(a, rows):
    return jnp.pad(a, ((0, rows - a.shape[0]), (0, 0)))


def _layer(h, meta, p, tm, tq):
    bsz, seq, d = h.shape
    g = lambda v: v.reshape(1, -1)
    w_a, w_q, w_kv = _mixer_in_weights(p["w_in"], p["w_uq"], p["w_ukv"])
    w1 = [g(p["ffn1_pre_g"]), _bf(p["ffn1_w_gate"]), _bf(p["ffn1_w_up"]), _bf(p["ffn1_w_down"]),
          g(p["ffn1_post_g"]), g(p["mix_pre_g"]), w_a, g(p["q_norm_g"]), w_q,
          g(p["kv_norm_g"]), w_kv]
    cs, sn = _rope_tables(N_META + jnp.arange(seq))
    cs_m, sn_m = _rope_tables(jnp.arange(N_META))

    h1, sbq, sbk, sbv, qm, km, vm = _ffn_proj_call(h.reshape(bsz * seq, d), cs, sn, w1, tm)
    _, _, sbk_m, sbv_m, _, km_m, vm_m = _ffn_proj_call(meta, cs_m, sn_m, w1, N_META)

    ysb = _sb_call(sbq, sbk, sbv, _pad_rows(sbk_m, META_PAD), _pad_rows(sbv_m, META_PAD),
                   bsz, seq, tq)
    ymla = _mla_call(qm, km, vm, _pad_rows(km_m, META_PAD), _pad_rows(vm_m, META_PAD),
                     bsz, seq, tq)

    w4 = [g(p["mix_pre_g"]), _bf(p["w_in"][:, 2208:]), g(p["b_gate"]), _bf(p["w_sb_o"]),
          _bf(p["w_mla_o"]), _bf(p["w_out"]), g(p["mix_post_g"]), g(p["ffn2_pre_g"]),
          _bf(p["ffn2_w_gate"]), _bf(p["ffn2_w_up"]), _bf(p["ffn2_w_down"]),
          g(p["ffn2_post_g"])]
    out = _merge_ffn_call(h1, ysb, ymla, w4, tm)
    return out.reshape(bsz, seq, d)


def kernel(x, meta_tokens, ffn1_pre_g, ffn1_w_gate, ffn1_w_up, ffn1_w_down, ffn1_post_g, mix_pre_g, w_in, b_gate, q_norm_g, w_uq, kv_norm_g, w_ukv, w_sb_o, w_mla_o, w_out, mix_post_g, ffn2_pre_g, ffn2_w_gate, ffn2_w_up, ffn2_w_down, ffn2_post_g):
    names = ("ffn1_pre_g", "ffn1_w_gate", "ffn1_w_up", "ffn1_w_down", "ffn1_post_g", "mix_pre_g",
             "w_in", "b_gate", "q_norm_g", "w_uq", "kv_norm_g", "w_ukv", "w_sb_o", "w_mla_o",
             "w_out", "mix_post_g", "ffn2_pre_g", "ffn2_w_gate", "ffn2_w_up", "ffn2_w_down",
             "ffn2_post_g")
    stacked = (ffn1_pre_g, ffn1_w_gate, ffn1_w_up, ffn1_w_down, ffn1_post_g, mix_pre_g, w_in,
               b_gate, q_norm_g, w_uq, kv_norm_g, w_ukv, w_sb_o, w_mla_o, w_out, mix_post_g,
               ffn2_pre_g, ffn2_w_gate, ffn2_w_up, ffn2_w_down, ffn2_post_g)
    assert all(w.shape[0] == 1 for w in stacked), "single-layer block"
    seq = x.shape[1]
    tq = 2 * SB_TK
    tm = 512
    assert seq % tq == 0 and (x.shape[0] * seq) % tm == 0 and seq % tm == 0
    p = {n: w[0] for n, w in zip(names, stacked)}
    return _layer(x, meta_tokens.astype(x.dtype), p, tm, tq)
```

```python
import functools

import jax
import jax.numpy as jnp
from jax import lax
from jax.experimental import pallas as pl
from jax.experimental.pallas import tpu as pltpu

D_MODEL = 1024
D_FF = 2816
N_META = 16
CHUNK = 64
SB_WIDTH = 512
MLA_HEADS = 8
MLA_Q_LORA = 384
MLA_KV_LORA = 256
MLA_NOPE = 64
MLA_ROPE = 32
MLA_V = 64
ROPE_THETA = 10000.0
EPS = 1e-6
NEG_INF = -1e30
LOG2_E = 1.4426950408889634
HEAD_PAIR = 128
MLA_QK_PAD = 128
META_PAD = 128
SB_TK = 256
ROW_GROUPS = 2
MIN_GROUP_ROWS = 128
ATTENTION_STARTS = (4, 3)
VMEM_LIMIT = 56 * 1024 * 1024

F32 = jnp.float32
BF16 = jnp.bfloat16


def _bf(x):
    return x.astype(BF16)


def _dot(a, b):
    return jnp.dot(a, b, preferred_element_type=F32)


def _dot_nt(a, b):
    return lax.dot_general(a, b, (((1,), (1,)), ((), ())), preferred_element_type=F32)


def _rms(x, g):
    return x * lax.rsqrt(jnp.mean(x * x, axis=-1, keepdims=True) + EPS) * g


def _row_stat(x):
    return jnp.broadcast_to(x, (x.shape[0], HEAD_PAIR))


def _across_lanes(stat, width):
    return stat if width == HEAD_PAIR else jnp.concatenate([stat] * (width // HEAD_PAIR), axis=1)


def _trace_pipelined(stage_gens, on_done, starts_per_step=(1, 1)):
    n, d = starts_per_step
    done = [False] * len(stage_gens)
    step = 0
    while not all(done):
        for g, gen in enumerate(stage_gens):
            if step >= g * d // n and not done[g]:
                try:
                    next(gen)
                except StopIteration as stop:
                    done[g] = True
                    on_done(g, stop.value)
        step += 1


def _ffn(h, pre_g, wg_ref, wu_ref, wd_ref, post_g):
    xn = _bf(_rms(h, pre_g))
    yield
    g = _dot(xn, wg_ref[...])
    u = _dot(xn, wu_ref[...])
    yield
    f = _dot(_bf(g * jax.nn.sigmoid(g) * u), wd_ref[...])
    yield
    return h + 0.5 * _rms(f, post_g)


def _trace_row_groups(n_rows, chain):
    rows = n_rows // ROW_GROUPS if n_rows % (ROW_GROUPS * MIN_GROUP_ROWS) == 0 else n_rows
    _trace_pipelined([chain(r, rows) for r in range(0, n_rows, rows)], lambda g, value: None)


def _ffn_proj_kernel(x_ref, cs_ref, sn_ref, pre_g, wg, wu, wd, post_g, mix_g, w_a, qn_g, w_q,
                     kvn_g, w_kv, h1_ref, sbq_ref, sbk_ref, sbv_ref, qm_ref, km_ref, vm_ref):
    scale = (MLA_NOPE + MLA_ROPE) ** -0.5 * LOG2_E
    half = MLA_HEADS * MLA_QK_PAD

    def chain(r, rows):
        rs = slice(r, r + rows)
        h1 = yield from _ffn(x_ref[rs, :], pre_g[...], wg, wu, wd, post_g[...])
        h1_ref[rs, :] = h1
        u = _bf(_rms(h1, mix_g[...]))
        yield
        p = _dot(u, w_a[...])
        sbq_ref[rs, :] = _bf(p[:, 0:512])
        sbk_ref[rs, :] = _bf(p[:, 512:1024])
        sbv_ref[rs, :] = _bf(p[:, 1024:1536])
        c_q = _bf(_rms(p[:, 1536:1920], qn_g[...]))
        c_kv = _bf(_rms(p[:, 1920:2176], kvn_g[...]))
        kr = p[:, 2176:2304]
        kr_rot = p[:, 2304:2432]
        yield
        q2 = _dot(c_q, w_q[...])
        kv = _dot(c_kv, w_kv[...])
        yield
        cs = cs_ref[rs, :]
        sn = sn_ref[rs, :]
        k_rope = kr * cs + kr_rot * sn
        for h in range(MLA_HEADS):
            lo, hi = h * MLA_QK_PAD, (h + 1) * MLA_QK_PAD
            qm_ref[rs, lo:hi] = _bf((q2[:, lo:hi] * cs + q2[:, half + lo:half + hi] * sn) * scale)
            km_ref[rs, lo:hi] = _bf(kv[:, lo:hi] + k_rope)
        vm_ref[rs, :] = _bf(kv[:, half:])

    _trace_row_groups(x_ref.shape[0], chain)


def _const_spec(shape):
    return pl.BlockSpec(shape, lambda *_: (0,) * len(shape), pipeline_mode=pl.Buffered(1))


def _ffn_proj_call(x2d, cs, sn, weights, tm):
    t = x2d.shape[0]
    n_pos_blocks = cs.shape[0] // tm
    row = lambda w: pl.BlockSpec((tm, w), lambda i: (i, 0))
    pos = pl.BlockSpec((tm, MLA_QK_PAD), lambda i: (i % n_pos_blocks, 0))
    in_specs = [row(D_MODEL), pos, pos] + [_const_spec(w.shape) for w in weights]
    widths = (D_MODEL, SB_WIDTH, SB_WIDTH, SB_WIDTH, MLA_HEADS * MLA_QK_PAD,
              MLA_HEADS * MLA_QK_PAD, MLA_HEADS * MLA_V)
    dtypes = (F32,) + (BF16,) * 6
    return pl.pallas_call(
        _ffn_proj_kernel,
        grid=(t // tm,),
        in_specs=in_specs,
        out_specs=[row(w) for w in widths],
        out_shape=[jax.ShapeDtypeStruct((t, w), d) for w, d in zip(widths, dtypes)],
        compiler_params=pltpu.CompilerParams(dimension_semantics=("parallel",),
                                             vmem_limit_bytes=VMEM_LIMIT),
        name="ffn1_proj",
    )(x2d, cs, sn, *weights)


def _sb_tile(qh, load_kv, ntri, mask):
    kblk, vblk = load_kv()
    z2 = _dot_nt(qh, kblk)
    sp = jnp.maximum(z2, 0.0) + jnp.log2(1.0 + jnp.exp2(-jnp.abs(z2)))
    log2_b = z2 - sp
    if mask is not None:
        sp = jnp.where(mask, sp, 0.0)
    sp_sum = _row_stat(jnp.sum(sp, axis=-1, keepdims=True))
    sp = _bf(sp)
    yield
    a = jnp.exp2(log2_b + _dot(sp, ntri))
    if mask is not None:
        a = jnp.where(mask, a, 0.0)
    a = _bf(a)
    yield
    return _dot(a, vblk), sp_sum


def _trace_query_tiles(query_tile, nq, tq, q_ref, o_ref):
    tiles, finishes = [], []
    for n in range(nq):
        rows = slice(n * tq, (n + 1) * tq)
        new_tiles, finish = query_tile(n, q_ref[rows, :])
        tiles += new_tiles
        finishes.append((rows, finish))
    _trace_pipelined([gen for gen, _ in tiles], lambda g, value: tiles[g][1](value),
                     ATTENTION_STARTS)
    for rows, finish in finishes:
        o_ref[rows, :] = finish()


def _sb_kernel(q_ref, k_ref, v_ref, mk_ref, mv_ref, ntri_ref, o_ref, *, tq, nq):
    _trace_query_tiles(
        lambda n, q: _sb_query_tile(n, q, k_ref, v_ref, mk_ref, mv_ref, ntri_ref, tq),
        nq, tq, q_ref, o_ref)


def _sb_query_tile(n_blocks, q, k_ref, v_ref, mk_ref, mv_ref, ntri_ref, tq):
    tk = SB_TK
    assert tq == 2 * tk
    start = n_blocks * tq
    lane = lax.broadcasted_iota(jnp.int32, (tq, HEAD_PAIR), 1)
    row = lax.broadcasted_iota(jnp.int32, (tq, tk), 0)
    col = lax.broadcasted_iota(jnp.int32, (tq, tk), 1)
    far_mask = (row >= tk) | (col < row)
    near_mask = lax.broadcasted_iota(jnp.int32, (tk, tk), 1) < lax.broadcasted_iota(
        jnp.int32, (tk, tk), 0)
    meta_mask = lax.broadcasted_iota(jnp.int32, (tq, META_PAD), 1) < N_META
    ntri = ntri_ref[...]
    zero = jnp.zeros_like(q)
    qs = [jnp.where(lane < 64, q, zero), jnp.where(lane >= 64, q, zero)]

    def kv(s):
        return lambda: (k_ref[pl.ds(s, tk), :], v_ref[pl.ds(s, tk), :])

    def sweep(qh):
        yield _sb_tile(qh[tk:], kv(start + tk), ntri, near_mask)
        yield _sb_tile(qh, kv(start), ntri, far_mask)
        for s in range(start - tk, -1, -tk):
            yield _sb_tile(qh, kv(s), ntri, None)
        yield _sb_tile(qh, lambda: (mk_ref[...], mv_ref[...]), ntri[:META_PAD, :META_PAD],
                       meta_mask)

    states = [None, None]

    def add(hh, result):
        pv, sp_sum = result
        if states[hh] is None:
            top = jnp.zeros((tk, HEAD_PAIR), F32)
            states[hh] = (jnp.concatenate([top, pv], axis=0),
                          jnp.concatenate([top, -sp_sum], axis=0))
        else:
            acc, log2_gap = states[hh]
            states[hh] = (acc + jnp.exp2(log2_gap) * pv, log2_gap - sp_sum)

    tiles = [(gen, functools.partial(add, hh))
             for pair in zip(sweep(qs[0]), sweep(qs[1])) for hh, gen in enumerate(pair)]
    return tiles, lambda: _bf(jnp.where(lane < 64, states[0][0], states[1][0]))


def _sb_call(sbq, sbk, sbv, mk, mv, bsz, seq, tq):
    nq = seq // tq
    ntri = -(lax.broadcasted_iota(jnp.int32, (SB_TK, SB_TK), 0)
             > lax.broadcasted_iota(jnp.int32, (SB_TK, SB_TK), 1)).astype(BF16)
    seqspec = pl.BlockSpec((seq, HEAD_PAIR), lambda b, h: (b, h))
    mspec = pl.BlockSpec((META_PAD, HEAD_PAIR), lambda b, h: (0, h))
    return pl.pallas_call(
        functools.partial(_sb_kernel, tq=tq, nq=nq),
        grid=(bsz, SB_WIDTH // HEAD_PAIR),
        in_specs=[seqspec, seqspec, seqspec, mspec, mspec,
                  pl.BlockSpec((SB_TK, SB_TK), lambda b, h: (0, 0))],
        out_specs=seqspec,
        out_shape=jax.ShapeDtypeStruct(sbq.shape, BF16),
        compiler_params=pltpu.CompilerParams(
            dimension_semantics=("parallel", "parallel"),
            vmem_limit_bytes=VMEM_LIMIT),
        name="sb_attention",
    )(sbq, sbk, sbv, mk, mv, ntri)


def _softmax_step(q, load_blocks, read_state):
    blocks = load_blocks()
    scores = []
    for kblk, _, mask in blocks:
        s = _dot_nt(q, kblk)
        scores.append(s if mask is None else jnp.where(mask, s, NEG_INF))
    m_new = functools.reduce(jnp.maximum, [_row_stat(jnp.max(s, axis=-1, keepdims=True))
                                           for s in scores])
    yield
    state = read_state()
    if state is not None:
        m, l, acc = state
        m_new = jnp.maximum(m, m_new)
        alpha = jnp.exp2(m - m_new)
    probs = [jnp.exp2(s - _across_lanes(m_new, s.shape[1])) for s in scores]
    l_terms = [_row_stat(jnp.sum(p, axis=-1, keepdims=True)) for p in probs]
    probs = [_bf(p) for p in probs]
    yield
    pv_terms = [_dot(p, vblk) for p, (_, vblk, _) in zip(probs, blocks)]
    if state is not None:
        l_terms.insert(0, alpha * l)
        pv_terms.insert(0, alpha * acc)
    return (m_new, functools.reduce(lambda a, b: a + b, l_terms),
            functools.reduce(lambda a, b: a + b, pv_terms))


def _mla_kernel(q_ref, k_ref, v_ref, mk_ref, mv_ref, o_ref, *, tq, nq):
    _trace_query_tiles(lambda n, q: _mla_query_tile(n, q, k_ref, v_ref, mk_ref, mv_ref, tq),
                       nq, tq, q_ref, o_ref)


def _mla_query_tile(n_blocks, q, k_ref, v_ref, mk_ref, mv_ref, tq):
    half = tq // 2
    start = n_blocks * tq
    lane = lax.broadcasted_iota(jnp.int32, (tq, HEAD_PAIR), 1)
    row = lax.broadcasted_iota(jnp.int32, (half, tq), 0)
    col = lax.broadcasted_iota(jnp.int32, (half, tq), 1)
    top_mask = ((col // CHUNK) <= (row // CHUNK))[:, :half]
    bot_mask = (col // CHUNK) <= ((row + half) // CHUNK)
    meta_mask = lax.broadcasted_iota(jnp.int32, (half, META_PAD), 1) < N_META
    lanes = [(hh * MLA_QK_PAD, (hh + 1) * MLA_QK_PAD) for hh in range(2)]
    qs = [q[:, lo:hi] for lo, hi in lanes]

    states = [{}, {}]

    def diag(hh, part, rows, keys, mask):
        lo, hi = lanes[hh]
        load = lambda: [(mk_ref[:, lo:hi], mv_ref[...], meta_mask),
                        (k_ref[pl.ds(start, keys), lo:hi], v_ref[pl.ds(start, keys), :], mask)]
        return part, _softmax_step(qs[hh][rows], load, lambda: None)

    def joined(hh):
        return tuple(jnp.concatenate([t, b], axis=0)
                     for t, b in zip(states[hh]["top"], states[hh]["bot"]))

    def block(hh, j):
        lo, hi = lanes[hh]
        load = lambda: [(k_ref[pl.ds(j * tq, tq), lo:hi], v_ref[pl.ds(j * tq, tq), :], None)]
        read = (lambda: joined(hh)) if j == 0 else (lambda: states[hh]["all"])
        return "all", _softmax_step(qs[hh], load, read)

    per_head = [[diag(hh, "top", slice(0, half), half, top_mask),
                 diag(hh, "bot", slice(half, tq), tq, bot_mask)]
                + [block(hh, j) for j in range(n_blocks)] for hh in range(2)]

    def store(hh, part, state):
        states[hh][part] = state

    tiles = [(gen, functools.partial(store, hh, part))
             for pair in zip(*per_head) for hh, (part, gen) in enumerate(pair)]

    def finish():
        outs = []
        for hh in range(2):
            _, l, acc = states[hh]["all"] if n_blocks else joined(hh)
            outs.append(acc / l)
        return _bf(jnp.where(lane < 64, outs[0], outs[1]))

    return tiles, finish


def _mla_call(qm, km, vm, mk, mv, bsz, seq, tq):
    nq = seq // tq
    pair = 2 * MLA_QK_PAD
    return pl.pallas_call(
        functools.partial(_mla_kernel, tq=tq, nq=nq),
        grid=(bsz, MLA_HEADS // 2),
        in_specs=[pl.BlockSpec((seq, pair), lambda b, h: (b, h)),
                  pl.BlockSpec((seq, pair), lambda b, h: (b, h)),
                  pl.BlockSpec((seq, HEAD_PAIR), lambda b, h: (b, h)),
                  pl.BlockSpec((META_PAD, pair), lambda b, h: (0, h)),
                  pl.BlockSpec((META_PAD, HEAD_PAIR), lambda b, h: (0, h))],
        out_specs=pl.BlockSpec((seq, HEAD_PAIR), lambda b, h: (b, h)),
        out_shape=jax.ShapeDtypeStruct(vm.shape, BF16),
        compiler_params=pltpu.CompilerParams(
            dimension_semantics=("parallel", "parallel"),
            vmem_limit_bytes=VMEM_LIMIT),
        name="mla_attention",
    )(qm, km, vm, mk, mv)


def _merge_ffn_kernel(h1_ref, ysb_ref, ymla_ref, mix_g, w_gate, b_gate, w_sbo, w_mlao, w_out,
                      mixp_g, pre_g, wg, wu, wd, post_g, o_ref):
    def chain(r, rows):
        rs = slice(r, r + rows)
        h1 = h1_ref[rs, :]
        u = _bf(_rms(h1, mix_g[...]))
        yield
        gate = jax.nn.sigmoid(_dot(u, w_gate[...]) + b_gate[...])
        y_sb = _dot(ysb_ref[rs, :], w_sbo[...])
        y_mla = _dot(ymla_ref[rs, :], w_mlao[...])
        yield
        merged = gate[:, :D_MODEL] * y_sb + gate[:, D_MODEL:] * y_mla
        m = _dot(_bf(merged), w_out[...])
        yield
        h2 = h1 + _rms(m, mixp_g[...])
        o_ref[rs, :] = yield from _ffn(h2, pre_g[...], wg, wu, wd, post_g[...])

    _trace_row_groups(h1_ref.shape[0], chain)


def _merge_ffn_call(h1, ysb, ymla, weights, tm):
    t = h1.shape[0]
    row = lambda w: pl.BlockSpec((tm, w), lambda i: (i, 0))
    return pl.pallas_call(
        _merge_ffn_kernel,
        grid=(t // tm,),
        in_specs=[row(D_MODEL), row(SB_WIDTH), row(SB_WIDTH)]
                 + [_const_spec(w.shape) for w in weights],
        out_specs=row(D_MODEL),
        out_shape=jax.ShapeDtypeStruct((t, D_MODEL), F32),
        compiler_params=pltpu.CompilerParams(dimension_semantics=("parallel",),
                                             vmem_limit_bytes=VMEM_LIMIT),
        name="merge_ffn2",
    )(h1, ysb, ymla, *weights)


def _rope_tables(pos):
    half = MLA_ROPE // 2
    inv = ROPE_THETA ** (-jnp.arange(half, dtype=F32) / half)
    ang = pos.astype(F32)[:, None] * inv[None, :]
    cos, sin = jnp.cos(ang), jnp.sin(ang)
    n = pos.shape[0]
    pad = jnp.zeros((n, MLA_QK_PAD - MLA_NOPE - MLA_ROPE), F32)
    cs = jnp.concatenate([jnp.ones((n, MLA_NOPE), F32), cos, cos, pad], axis=1)
    sn = jnp.concatenate([jnp.zeros((n, MLA_NOPE), F32), sin, sin, pad], axis=1)
    return cs, sn


def _rotate_half_cols(w):
    half = MLA_ROPE // 2
    return jnp.concatenate([-w[..., half:], w[..., :half]], axis=-1)


def _mixer_in_weights(w_in, w_uq, w_ukv):
    d = w_in.shape[0]
    w_sbq = w_in[:, 0:512] * (64 ** -0.5 * LOG2_E)
    w_kr = w_in[:, 2176:2208]
    z_nope = jnp.zeros((d, MLA_NOPE), F32)
    z_pad = jnp.zeros((d, MLA_QK_PAD - MLA_NOPE - MLA_ROPE), F32)
    w_a = jnp.concatenate([w_sbq, w_in[:, 512:2176],
                           z_nope, w_kr, z_pad,
                           z_nope, _rotate_half_cols(w_kr), z_pad], axis=1)
    q_nope, q_rope = w_uq[:, :, :MLA_NOPE], w_uq[:, :, MLA_NOPE:]
    zq_nope = jnp.zeros_like(q_nope)
    zq_pad = jnp.zeros(q_rope.shape[:2] + (MLA_QK_PAD - MLA_NOPE - MLA_ROPE,), F32)
    w_q = jnp.concatenate([
        jnp.concatenate([q_nope, q_rope, zq_pad], -1).reshape(MLA_Q_LORA, -1),
        jnp.concatenate([zq_nope, _rotate_half_cols(q_rope), zq_pad], -1).reshape(MLA_Q_LORA, -1),
    ], axis=1)
    k_nope, v = w_ukv[:, :, :MLA_NOPE], w_ukv[:, :, MLA_NOPE:]
    w_kv = jnp.concatenate([
        jnp.concatenate([k_nope, jnp.zeros_like(k_nope)], -1).reshape(MLA_KV_LORA, -1),
        v.reshape(MLA_KV_LORA, -1)], axis=1)
    return _bf(w_a), _bf(w_q), _bf(w_kv)


def _pad_rows(a, rows):
    return jnp.pad(a, ((0, rows - a.shape[0]), (0, 0)))


def _layer(h, meta, p, tm, tq):
    bsz, seq, d = h.shape
    g = lambda v: v.reshape(1, -1)
    w_a, w_q, w_kv = _mixer_in_weights(p["w_in"], p["w_uq"], p["w_ukv"])
    w1 = [g(p["ffn1_pre_g"]), _bf(p["ffn1_w_gate"]), _bf(p["ffn1_w_up"]), _bf(p["ffn1_w_down"]),
          g(p["ffn1_post_g"]), g(p["mix_pre_g"]), w_a, g(p["q_norm_g"]), w_q,
          g(p["kv_norm_g"]), w_kv]
    cs, sn = _rope_tables(N_META + jnp.arange(seq))
    cs_m, sn_m = _rope_tables(jnp.arange(N_META))

    h1, sbq, sbk, sbv, qm, km, vm = _ffn_proj_call(h.reshape(bsz * seq, d), cs, sn, w1, tm)
    _, _, sbk_m, sbv_m, _, km_m, vm_m = _ffn_proj_call(meta, cs_m, sn_m, w1, N_META)

    ysb = _sb_call(sbq, sbk, sbv, _pad_rows(sbk_m, META_PAD), _pad_rows(sbv_m, META_PAD),
                   bsz, seq, tq)
    ymla = _mla_call(qm, km, vm, _pad_rows(km_m, META_PAD), _pad_rows(vm_m, META_PAD),
                     bsz, seq, tq)

    w4 = [g(p["mix_pre_g"]), _bf(p["w_in"][:, 2208:]), g(p["b_gate"]), _bf(p["w_sb_o"]),
          _bf(p["w_mla_o"]), _bf(p["w_out"]), g(p["mix_post_g"]), g(p["ffn2_pre_g"]),
          _bf(p["ffn2_w_gate"]), _bf(p["ffn2_w_up"]), _bf(p["ffn2_w_down"]),
          g(p["ffn2_post_g"])]
    out = _merge_ffn_call(h1, ysb, ymla, w4, tm)
    return out.reshape(bsz, seq, d)


def kernel(x, meta_tokens, ffn1_pre_g, ffn1_w_gate, ffn1_w_up, ffn1_w_down, ffn1_post_g, mix_pre_g, w_in, b_gate, q_norm_g, w_uq, kv_norm_g, w_ukv, w_sb_o, w_mla_o, w_out, mix_post_g, ffn2_pre_g, ffn2_w_gate, ffn2_w_up, ffn2_w_down, ffn2_post_g):
    names = ("ffn1_pre_g", "ffn1_w_gate", "ffn1_w_up", "ffn1_w_down", "ffn1_post_g", "mix_pre_g",
             "w_in", "b_gate", "q_norm_g", "w_uq", "kv_norm_g", "w_ukv", "w_sb_o", "w_mla_o",
             "w_out", "mix_post_g", "ffn2_pre_g", "ffn2_w_gate", "ffn2_w_up", "ffn2_w_down",
             "ffn2_post_g")
    stacked = (ffn1_pre_g, ffn1_w_gate, ffn1_w_up, ffn1_w_down, ffn1_post_g, mix_pre_g, w_in,
               b_gate, q_norm_g, w_uq, kv_norm_g, w_ukv, w_sb_o, w_mla_o, w_out, mix_post_g,
               ffn2_pre_g, ffn2_w_gate, ffn2_w_up, ffn2_w_down, ffn2_post_g)
    assert all(w.shape[0] == 1 for w in stacked), "single-layer block"
    seq = x.shape[1]
    tq = 2 * SB_TK
    tm = 512
    assert seq % tq == 0 and (x.shape[0] * seq) % tm == 0 and seq % tm == 0
    p = {n: w[0] for n, w in zip(names, stacked)}
    return _layer(x, meta_tokens.astype(x.dtype), p, tm, tq)
```

```python
import functools

import jax
import jax.numpy as jnp
from jax import lax
from jax.experimental import pallas as pl
from jax.experimental.pallas import tpu as pltpu

D_MODEL = 1024
D_FF = 2816
N_META = 16
CHUNK = 64
SB_WIDTH = 512
MLA_HEADS = 8
MLA_Q_LORA = 384
MLA_KV_LORA = 256
MLA_NOPE = 64
MLA_ROPE = 32
MLA_V = 64
ROPE_THETA = 10000.0
EPS = 1e-6
NEG_INF = -1e30
LOG2_E = 1.4426950408889634
HEAD_PAIR = 128
MLA_QK_PAD = 128
META_PAD = 128
SB_TK = 256
ROW_GROUPS = 2
MIN_GROUP_ROWS = 128
ATTENTION_STARTS = (4, 3)
VMEM_LIMIT = 56 * 1024 * 1024

F32 = jnp.float32
BF16 = jnp.bfloat16


def _bf(x):
    return x.astype(BF16)


def _dot(a, b):
    return jnp.dot(a, b, preferred_element_type=F32)


def _dot_nt(a, b):
    return lax.dot_general(a, b, (((1,), (1,)), ((), ())), preferred_element_type=F32)


def _rms(x, g):
    return x * lax.rsqrt(jnp.mean(x * x, axis=-1, keepdims=True) + EPS) * g


def _row_stat(x):
    return jnp.broadcast_to(x, (x.shape[0], HEAD_PAIR))


def _across_lanes(stat, width):
    return stat if width == HEAD_PAIR else jnp.concatenate([stat] * (width // HEAD_PAIR), axis=1)


def _trace_pipelined(stage_gens, on_done, starts_per_step=(1, 1)):
    n, d = starts_per_step
    done = [False] * len(stage_gens)
    step = 0
    while not all(done):
        for g, gen in enumerate(stage_gens):
            if step >= g * d // n and not done[g]:
                try:
                    next(gen)
                except StopIteration as stop:
                    done[g] = True
                    on_done(g, stop.value)
        step += 1


def _ffn(h, pre_g, wg_ref, wu_ref, wd_ref, post_g):
    xn = _bf(_rms(h, pre_g))
    yield
    g = _dot(xn, wg_ref[...])
    u = _dot(xn, wu_ref[...])
    yield
    f = _dot(_bf(g * jax.nn.sigmoid(g) * u), wd_ref[...])
    yield
    return h + 0.5 * _rms(f, post_g)


def _trace_row_groups(n_rows, chain):
    rows = n_rows // ROW_GROUPS if n_rows % (ROW_GROUPS * MIN_GROUP_ROWS) == 0 else n_rows
    _trace_pipelined([chain(r, rows) for r in range(0, n_rows, rows)], lambda g, value: None)


def _ffn_proj_kernel(x_ref, cs_ref, sn_ref, pre_g, wg, wu, wd, post_g, mix_g, w_a, qn_g, w_q,
                     kvn_g, w_kv, h1_ref, sbq_ref, sbk_ref, sbv_ref, qm_ref, km_ref, vm_ref):
    scale = (MLA_NOPE + MLA_ROPE) ** -0.5 * LOG2_E
    half = MLA_HEADS * MLA_QK_PAD

    def chain(r, rows):
        rs = slice(r, r + rows)
        h1 = yield from _ffn(x_ref[rs, :], pre_g[...], wg, wu, wd, post_g[...])
        h1_ref[rs, :] = h1
        u = _bf(_rms(h1, mix_g[...]))
        yield
        p = _dot(u, w_a[...])
        sbq_ref[rs, :] = _bf(p[:, 0:512])
        sbk_ref[rs, :] = _bf(p[:, 512:1024])
        sbv_ref[rs, :] = _bf(p[:, 1024:1536])
        c_q = _bf(_rms(p[:, 1536:1920], qn_g[...]))
        c_kv = _bf(_rms(p[:, 1920:2176], kvn_g[...]))
        kr = p[:, 2176:2304]
        kr_rot = p[:, 2304:2432]
        yield
        q2 = _dot(c_q, w_q[...])
        kv = _dot(c_kv, w_kv[...])
        yield
        cs = cs_ref[rs, :]
        sn = sn_ref[rs, :]
        k_rope = kr * cs + kr_rot * sn
        for h in range(MLA_HEADS):
            lo, hi = h * MLA_QK_PAD, (h + 1) * MLA_QK_PAD
            qm_ref[rs, lo:hi] = _bf((q2[:, lo:hi] * cs + q2[:, half + lo:half + hi] * sn) * scale)
            km_ref[rs, lo:hi] = _bf(kv[:, lo:hi] + k_rope)
        vm_ref[rs, :] = _bf(kv[:, half:])

    _trace_row_groups(x_ref.shape[0], chain)


def _const_spec(shape):
    return pl.BlockSpec(shape, lambda *_: (0,) * len(shape), pipeline_mode=pl.Buffered(1))


def _ffn_proj_call(x2d, cs, sn, weights, tm):
    t = x2d.shape[0]
    n_pos_blocks = cs.shape[0] // tm
    row = lambda w: pl.BlockSpec((tm, w), lambda i: (i, 0))
    pos = pl.BlockSpec((tm, MLA_QK_PAD), lambda i: (i % n_pos_blocks, 0))
    in_specs = [row(D_MODEL), pos, pos] + [_const_spec(w.shape) for w in weights]
    widths = (D_MODEL, SB_WIDTH, SB_WIDTH, SB_WIDTH, MLA_HEADS * MLA_QK_PAD,
              MLA_HEADS * MLA_QK_PAD, MLA_HEADS * MLA_V)
    dtypes = (F32,) + (BF16,) * 6
    return pl.pallas_call(
        _ffn_proj_kernel,
        grid=(t // tm,),
        in_specs=in_specs,
        out_specs=[row(w) for w in widths],
        out_shape=[jax.ShapeDtypeStruct((t, w), d) for w, d in zip(widths, dtypes)],
        compiler_params=pltpu.CompilerParams(dimension_semantics=("parallel",),
                                             vmem_limit_bytes=VMEM_LIMIT),
        name="ffn1_proj",
    )(x2d, cs, sn, *weights)


def _sb_tile(qh, load_kv, ntri, mask):
    kblk, vblk = load_kv()
    z2 = _dot_nt(qh, kblk)
    sp = jnp.maximum(z2, 0.0) + jnp.log2(1.0 + jnp.exp2(-jnp.abs(z2)))
    log2_b = z2 - sp
    if mask is not None:
        sp = jnp.where(mask, sp, 0.0)
    sp_sum = _row_stat(jnp.sum(sp, axis=-1, keepdims=True))
    sp = _bf(sp)
    yield
    a = jnp.exp2(log2_b + _dot(sp, ntri))
    if mask is not None:
        a = jnp.where(mask, a, 0.0)
    a = _bf(a)
    yield
    return _dot(a, vblk), sp_sum


def _trace_query_tiles(query_tile, nq, tq, q_ref, o_ref):
    tiles, finishes = [], []
    for n in range(nq):
        rows = slice(n * tq, (n + 1) * tq)
        new_tiles, finish = query_tile(n, q_ref[rows, :])
        tiles += new_tiles
        finishes.append((rows, finish))
    _trace_pipelined([gen for gen, _ in tiles], lambda g, value: tiles[g][1](value),
                     ATTENTION_STARTS)
    for rows, finish in finishes:
        o_ref[rows, :] = finish()


def _sb_kernel(q_ref, k_ref, v_ref, mk_ref, mv_ref, ntri_ref, o_ref, *, tq, nq):
    _trace_query_tiles(
        lambda n, q: _sb_query_tile(n, q, k_ref, v_ref, mk_ref, mv_ref, ntri_ref, tq),
        nq, tq, q_ref, o_ref)


def _sb_query_tile(n_blocks, q, k_ref, v_ref, mk_ref, mv_ref, ntri_ref, tq):
    tk = SB_TK
    assert tq == 2 * tk
    start = n_blocks * tq
    lane = lax.broadcasted_iota(jnp.int32, (tq, HEAD_PAIR), 1)
    row = lax.broadcasted_iota(jnp.int32, (tq, tk), 0)
    col = lax.broadcasted_iota(jnp.int32, (tq, tk), 1)
    far_mask = (row >= tk) | (col < row)
    near_mask = lax.broadcasted_iota(jnp.int32, (tk, tk), 1) < lax.broadcasted_iota(
        jnp.int32, (tk, tk), 0)
    meta_mask = lax.broadcasted_iota(jnp.int32, (tq, META_PAD), 1) < N_META
    ntri = ntri_ref[...]
    zero = jnp.zeros_like(q)
    qs = [jnp.where(lane < 64, q, zero), jnp.where(lane >= 64, q, zero)]

    def kv(s):
        return lambda: (k_ref[pl.ds(s, tk), :], v_ref[pl.ds(s, tk), :])

    def sweep(qh):
        yield _sb_tile(qh[tk:], kv(start + tk), ntri, near_mask)
        yield _sb_tile(qh, kv(start), ntri, far_mask)
        for s in range(start - tk, -1, -tk):
            yield _sb_tile(qh, kv(s), ntri, None)
        yield _sb_tile(qh, lambda: (mk_ref[...], mv_ref[...]), ntri[:META_PAD, :META_PAD],
                       meta_mask)

    states = [None, None]

    def add(hh, result):
        pv, sp_sum = result
        if states[hh] is None:
            top = jnp.zeros((tk, HEAD_PAIR), F32)
            states[hh] = (jnp.concatenate([top, pv], axis=0),
                          jnp.concatenate([top, -sp_sum], axis=0))
        else:
            acc, log2_gap = states[hh]
            states[hh] = (acc + jnp.exp2(log2_gap) * pv, log2_gap - sp_sum)

    tiles = [(gen, functools.partial(add, hh))
             for pair in zip(sweep(qs[0]), sweep(qs[1])) for hh, gen in enumerate(pair)]
    return tiles, lambda: _bf(jnp.where(lane < 64, states[0][0], states[1][0]))


def _sb_call(sbq, sbk, sbv, mk, mv, bsz, seq, tq):
    nq = seq // tq
    ntri = -(lax.broadcasted_iota(jnp.int32, (SB_TK, SB_TK), 0)
             > lax.broadcasted_iota(jnp.int32, (SB_TK, SB_TK), 1)).astype(BF16)
    seqspec = pl.BlockSpec((seq, HEAD_PAIR), lambda b, h: (b, h))
    mspec = pl.BlockSpec((META_PAD, HEAD_PAIR), lambda b, h: (0, h))
    return pl.pallas_call(
        functools.partial(_sb_kernel, tq=tq, nq=nq),
        grid=(bsz, SB_WIDTH // HEAD_PAIR),
        in_specs=[seqspec, seqspec, seqspec, mspec, mspec,
                  pl.BlockSpec((SB_TK, SB_TK), lambda b, h: (0, 0))],
        out_specs=seqspec,
        out_shape=jax.ShapeDtypeStruct(sbq.shape, BF16),
        compiler_params=pltpu.CompilerParams(
            dimension_semantics=("parallel", "parallel"),
            vmem_limit_bytes=VMEM_LIMIT),
        name="sb_attention",
    )(sbq, sbk, sbv, mk, mv, ntri)


def _softmax_step(q, load_blocks, read_state):
    blocks = load_blocks()
    scores = []
    for kblk, _, mask in blocks:
        s = _dot_nt(q, kblk)
        scores.append(s if mask is None else jnp.where(mask, s, NEG_INF))
    m_new = functools.reduce(jnp.maximum, [_row_stat(jnp.max(s, axis=-1, keepdims=True))
                                           for s in scores])
    yield
    state = read_state()
    if state is not None:
        m, l, acc = state
        m_new = jnp.maximum(m, m_new)
        alpha = jnp.exp2(m - m_new)
    probs = [jnp.exp2(s - _across_lanes(m_new, s.shape[1])) for s in scores]
    l_terms = [_row_stat(jnp.sum(p, axis=-1, keepdims=True)) for p in probs]
    probs = [_bf(p) for p in probs]
    yield
    pv_terms = [_dot(p, vblk) for p, (_, vblk, _) in zip(probs, blocks)]
    if state is not None:
        l_terms.insert(0, alpha * l)
        pv_terms.insert(0, alpha * acc)
    return (m_new, functools.reduce(lambda a, b: a + b, l_terms),
            functools.reduce(lambda a, b: a + b, pv_terms))


def _mla_kernel(q_ref, k_ref, v_ref, mk_ref, mv_ref, o_ref, *, tq, nq):
    _trace_query_tiles(lambda n, q: _mla_query_tile(n, q, k_ref, v_ref, mk_ref, mv_ref, tq),
                       nq, tq, q_ref, o_ref)


def _mla_query_tile(n_blocks, q, k_ref, v_ref, mk_ref, mv_ref, tq):
    half = tq // 2
    start = n_blocks * tq
    lane = lax.broadcasted_iota(jnp.int32, (tq, HEAD_PAIR), 1)
    row = lax.broadcasted_iota(jnp.int32, (half, tq), 0)
    col = lax.broadcasted_iota(jnp.int32, (half, tq), 1)
    top_mask = ((col // CHUNK) <= (row // CHUNK))[:, :half]
    bot_mask = (col // CHUNK) <= ((row + half) // CHUNK)
    meta_mask = lax.broadcasted_iota(jnp.int32, (half, META_PAD), 1) < N_META
    lanes = [(hh * MLA_QK_PAD, (hh + 1) * MLA_QK_PAD) for hh in range(2)]
    qs = [q[:, lo:hi] for lo, hi in lanes]

    states = [{}, {}]

    def diag(hh, part, rows, keys, mask):
        lo, hi = lanes[hh]
        load = lambda: [(mk_ref[:, lo:hi], mv_ref[...], meta_mask),
                        (k_ref[pl.ds(start, keys), lo:hi], v_ref[pl.ds(start, keys), :], mask)]
        return part, _softmax_step(qs[hh][rows], load, lambda: None)

    def joined(hh):
        return tuple(jnp.concatenate([t, b], axis=0)
                     for t, b in zip(states[hh]["top"], states[hh]["bot"]))

    def block(hh, part, rows, j):
        lo, hi = lanes[hh]
        load = lambda: [(k_ref[pl.ds(j * tq, tq), lo:hi], v_ref[pl.ds(j * tq, tq), :], None)]
        return part, _softmax_step(qs[hh][rows], load, lambda: states[hh][part])

    halves = (("top", slice(0, half)), ("bot", slice(half, tq)))
    per_head = [[diag(hh, "top", slice(0, half), half, top_mask),
                 diag(hh, "bot", slice(half, tq), tq, bot_mask)]
                + [block(hh, part, rows, j) for j in range(n_blocks) for part, rows in halves]
                for hh in range(2)]

    def store(hh, part, state):
        states[hh][part] = state

    tiles = [(gen, functools.partial(store, hh, part))
             for pair in zip(*per_head) for hh, (part, gen) in enumerate(pair)]

    def finish():
        outs = []
        for hh in range(2):
            _, l, acc = joined(hh)
            outs.append(acc / l)
        return _bf(jnp.where(lane < 64, outs[0], outs[1]))

    return tiles, finish


def _mla_call(qm, km, vm, mk, mv, bsz, seq, tq):
    nq = seq // tq
    pair = 2 * MLA_QK_PAD
    return pl.pallas_call(
        functools.partial(_mla_kernel, tq=tq, nq=nq),
        grid=(bsz, MLA_HEADS // 2),
        in_specs=[pl.BlockSpec((seq, pair), lambda b, h: (b, h)),
                  pl.BlockSpec((seq, pair), lambda b, h: (b, h)),
                  pl.BlockSpec((seq, HEAD_PAIR), lambda b, h: (b, h)),
                  pl.BlockSpec((META_PAD, pair), lambda b, h: (0, h)),
                  pl.BlockSpec((META_PAD, HEAD_PAIR), lambda b, h: (0, h))],
        out_specs=pl.BlockSpec((seq, HEAD_PAIR), lambda b, h: (b, h)),
        out_shape=jax.ShapeDtypeStruct(vm.shape, BF16),
        compiler_params=pltpu.CompilerParams(
            dimension_semantics=("parallel", "parallel"),
            vmem_limit_bytes=VMEM_LIMIT),
        name="mla_attention",
    )(qm, km, vm, mk, mv)


def _merge_ffn_kernel(h1_ref, ysb_ref, ymla_ref, mix_g, w_gate, b_gate, w_sbo, w_mlao, w_out,
                      mixp_g, pre_g, wg, wu, wd, post_g, o_ref):
    def chain(r, rows):
        rs = slice(r, r + rows)
        h1 = h1_ref[rs, :]
        u = _bf(_rms(h1, mix_g[...]))
        yield
        gate = jax.nn.sigmoid(_dot(u, w_gate[...]) + b_gate[...])
        y_sb = _dot(ysb_ref[rs, :], w_sbo[...])
        y_mla = _dot(ymla_ref[rs, :], w_mlao[...])
        yield
        merged = gate[:, :D_MODEL] * y_sb + gate[:, D_MODEL:] * y_mla
        m = _dot(_bf(merged), w_out[...])
        yield
        h2 = h1 + _rms(m, mixp_g[...])
        o_ref[rs, :] = yield from _ffn(h2, pre_g[...], wg, wu, wd, post_g[...])

    _trace_row_groups(h1_ref.shape[0], chain)


def _merge_ffn_call(h1, ysb, ymla, weights, tm):
    t = h1.shape[0]
    row = lambda w: pl.BlockSpec((tm, w), lambda i: (i, 0))
    return pl.pallas_call(
        _merge_ffn_kernel,
        grid=(t // tm,),
        in_specs=[row(D_MODEL), row(SB_WIDTH), row(SB_WIDTH)]
                 + [_const_spec(w.shape) for w in weights],
        out_specs=row(D_MODEL),
        out_shape=jax.ShapeDtypeStruct((t, D_MODEL), F32),
        compiler_params=pltpu.CompilerParams(dimension_semantics=("parallel",),
                                             vmem_limit_bytes=VMEM_LIMIT),
        name="merge_ffn2",
    )(h1, ysb, ymla, *weights)


def _rope_tables(pos):
    half = MLA_ROPE // 2
    inv = ROPE_THETA ** (-jnp.arange(half, dtype=F32) / half)
    ang = pos.astype(F32)[:, None] * inv[None, :]
    cos, sin = jnp.cos(ang), jnp.sin(ang)
    n = pos.shape[0]
    pad = jnp.zeros((n, MLA_QK_PAD - MLA_NOPE - MLA_ROPE), F32)
    cs = jnp.concatenate([jnp.ones((n, MLA_NOPE), F32), cos, cos, pad], axis=1)
    sn = jnp.concatenate([jnp.zeros((n, MLA_NOPE), F32), sin, sin, pad], axis=1)
    return cs, sn


def _rotate_half_cols(w):
    half = MLA_ROPE // 2
    return jnp.concatenate([-w[..., half:], w[..., :half]], axis=-1)


def _mixer_in_weights(w_in, w_uq, w_ukv):
    d = w_in.shape[0]
    w_sbq = w_in[:, 0:512] * (64 ** -0.5 * LOG2_E)
    w_kr = w_in[:, 2176:2208]
    z_nope = jnp.zeros((d, MLA_NOPE), F32)
    z_pad = jnp.zeros((d, MLA_QK_PAD - MLA_NOPE - MLA_ROPE), F32)
    w_a = jnp.concatenate([w_sbq, w_in[:, 512:2176],
                           z_nope, w_kr, z_pad,
                           z_nope, _rotate_half_cols(w_kr), z_pad], axis=1)
    q_nope, q_rope = w_uq[:, :, :MLA_NOPE], w_uq[:, :, MLA_NOPE:]
    zq_nope = jnp.zeros_like(q_nope)
    zq_pad = jnp.zeros(q_rope.shape[:2] + (MLA_QK_PAD - MLA_NOPE - MLA_ROPE,), F32)
    w_q = jnp.concatenate([
        jnp.concatenate([q_nope, q_rope, zq_pad], -1).reshape(MLA_Q_LORA, -1),
        jnp.concatenate([zq_nope, _rotate_half_cols(q_rope), zq_pad], -1).reshape(MLA_Q_LORA, -1),
    ], axis=1)
    k_nope, v = w_ukv[:, :, :MLA_NOPE], w_ukv[:, :, MLA_NOPE:]
    w_kv = jnp.concatenate([
        jnp.concatenate([k_nope, jnp.zeros_like(k_nope)], -1).reshape(MLA_KV_LORA, -1),
        v.reshape(MLA_KV_LORA, -1)], axis=1)
    return _bf(w_a), _bf(w_q), _bf(w_kv)


def _pad_rows(a, rows):
    return jnp.pad(a, ((0, rows - a.shape[0]), (0, 0)))


def _layer(h, meta, p, tm, tq):
    bsz, seq, d = h.shape
    g = lambda v: v.reshape(1, -1)
    w_a, w_q, w_kv = _mixer_in_weights(p["w_in"], p["w_uq"], p["w_ukv"])
    w1 = [g(p["ffn1_pre_g"]), _bf(p["ffn1_w_gate"]), _bf(p["ffn1_w_up"]), _bf(p["ffn1_w_down"]),
          g(p["ffn1_post_g"]), g(p["mix_pre_g"]), w_a, g(p["q_norm_g"]), w_q,
          g(p["kv_norm_g"]), w_kv]
    cs, sn = _rope_tables(N_META + jnp.arange(seq))
    cs_m, sn_m = _rope_tables(jnp.arange(N_META))

    h1, sbq, sbk, sbv, qm, km, vm = _ffn_proj_call(h.reshape(bsz * seq, d), cs, sn, w1, tm)
    _, _, sbk_m, sbv_m, _, km_m, vm_m = _ffn_proj_call(meta, cs_m, sn_m, w1, N_META)

    ysb = _sb_call(sbq, sbk, sbv, _pad_rows(sbk_m, META_PAD), _pad_rows(sbv_m, META_PAD),
                   bsz, seq, tq)
    ymla = _mla_call(qm, km, vm, _pad_rows(km_m, META_PAD), _pad_rows(vm_m, META_PAD),
                     bsz, seq, tq)

    w4 = [g(p["mix_pre_g"]), _bf(p["w_in"][:, 2208:]), g(p["b_gate"]), _bf(p["w_sb_o"]),
          _bf(p["w_mla_o"]), _bf(p["w_out"]), g(p["mix_post_g"]), g(p["ffn2_pre_g"]),
          _bf(p["ffn2_w_gate"]), _bf(p["ffn2_w_up"]), _bf(p["ffn2_w_down"]),
          g(p["ffn2_post_g"])]
    out = _merge_ffn_call(h1, ysb, ymla, w4, tm)
    return out.reshape(bsz, seq, d)


def kernel(x, meta_tokens, ffn1_pre_g, ffn1_w_gate, ffn1_w_up, ffn1_w_down, ffn1_post_g, mix_pre_g, w_in, b_gate, q_norm_g, w_uq, kv_norm_g, w_ukv, w_sb_o, w_mla_o, w_out, mix_post_g, ffn2_pre_g, ffn2_w_gate, ffn2_w_up, ffn2_w_down, ffn2_post_g):
    names = ("ffn1_pre_g", "ffn1_w_gate", "ffn1_w_up", "ffn1_w_down", "ffn1_post_g", "mix_pre_g",
             "w_in", "b_gate", "q_norm_g", "w_uq", "kv_norm_g", "w_ukv", "w_sb_o", "w_mla_o",
             "w_out", "mix_post_g", "ffn2_pre_g", "ffn2_w_gate", "ffn2_w_up", "ffn2_w_down",
             "ffn2_post_g")
    stacked = (ffn1_pre_g, ffn1_w_gate, ffn1_w_up, ffn1_w_down, ffn1_post_g, mix_pre_g, w_in,
               b_gate, q_norm_g, w_uq, kv_norm_g, w_ukv, w_sb_o, w_mla_o, w_out, mix_post_g,
               ffn2_pre_g, ffn2_w_gate, ffn2_w_up, ffn2_w_down, ffn2_post_g)
    assert all(w.shape[0] == 1 for w in stacked), "single-layer block"
    seq = x.shape[1]
    tq = 2 * SB_TK
    tm = 512
    assert seq % tq == 0 and (x.shape[0] * seq) % tm == 0 and seq % tm == 0
    p = {n: w[0] for n, w in zip(names, stacked)}
    return _layer(x, meta_tokens.astype(x.dtype), p, tm, tq)
```
